```python
import jax, jax.numpy as jnp
from jax import lax
import numpy as np

D_MODEL = 2048
BATCH = 2
SEQ = 8192
DEPTH = 4

CHUNK = 64
N_MIXERS = 3
N_RWKV = (DEPTH + 2) // N_MIXERS
N_RET = (DEPTH + 1) // N_MIXERS
N_MLSTM = DEPTH // N_MIXERS
NORM_EPS = 1e-6

RW_HEAD = 64
RW_HEADS = D_MODEL // RW_HEAD
RW_DECAY_LORA = max(32, int(round(1.8 * D_MODEL ** 0.5 / 32)) * 32)
RW_AAA_LORA = max(32, int(round(1.8 * D_MODEL ** 0.5 / 32)) * 32)
RW_MV_LORA = max(32, int(round(1.3 * D_MODEL ** 0.5 / 32)) * 32)
RW_GATE_LORA = max(32, int(round(0.6 * D_MODEL ** 0.8 / 32)) * 32)
RW_GN_EPS = 64e-5

RET_HEADS = 8
RET_DK = D_MODEL // RET_HEADS
RET_DV = 2 * D_MODEL // RET_HEADS
ROPE_BASE = 10000.0

ML_HEADS = 4
ML_DQK = D_MODEL // 2 // ML_HEADS
ML_DV = D_MODEL // ML_HEADS
ML_GATE_CAP = 15.0

D_FF = -(-8 * D_MODEL // (3 * 256)) * 256

kernel_name = 'hybrid_rwkv7_retention_mlstm_trunk'

F32 = jnp.float32


def _rms_norm(x, g):
    xf = x.astype(F32)
    y = xf * lax.rsqrt(jnp.mean(xf * xf, axis=-1, keepdims=True) + NORM_EPS)
    return (y * g.astype(F32)).astype(x.dtype)


def _head_layer_norm(y, eps):
    mean = jnp.mean(y, axis=-1, keepdims=True)
    yc = y - mean
    return yc * lax.rsqrt(jnp.mean(yc * yc, axis=-1, keepdims=True) + eps)


def _to_chunks(t):
    b, s = t.shape[:2]
    return jnp.moveaxis(t.reshape(b, s // CHUNK, CHUNK, *t.shape[2:]), 1, 0)


def _from_chunks(t):
    t = jnp.moveaxis(t, 0, 1)
    return t.reshape(t.shape[0], t.shape[1] * t.shape[2], *t.shape[3:])


def _rotary_tables(positions):
    inv_freq = 1.0 / (ROPE_BASE ** jnp.linspace(0.0, 1.0, RET_DK // 2, dtype=F32))
    ang = positions.astype(F32)[..., None] * inv_freq
    return jnp.cos(ang), jnp.sin(ang)


def _rotary(t, cos, sin):
    half = t.shape[-1] // 2
    t1, t2 = t[..., :half], t[..., half:]
    c = cos[:, :, None, :].astype(t.dtype)
    s = sin[:, :, None, :].astype(t.dtype)
    return jnp.concatenate([t1 * c - t2 * s, t1 * s + t2 * c], axis=-1)


def _rwkv7_scan(r, w, k, v, a, b):
    bsz, _, nh, n = r.shape

    def step(st, inp):
        r_t, w_t, k_t, v_t, a_t, b_t = inp
        sa = jnp.einsum('bhvk,bhk->bhv', st, a_t)
        st = st * w_t[:, :, None, :] + sa[..., None] * b_t[:, :, None, :] + v_t[..., None] * k_t[:, :, None, :]
        return st, jnp.einsum('bhvk,bhk->bhv', st, r_t)

    xs = tuple(jnp.moveaxis(t, 1, 0) for t in (r, w, k, v, a, b))
    _, ys = lax.scan(step, jnp.zeros((bsz, nh, n, n), F32), xs)
    return jnp.moveaxis(ys, 0, 1)


def _rwkv7_mix(h, v_first, mu, w_rkv, w0, w1, w2, a0, a1, a2, g1, g2, k_k, k_a, r_k, ln_w, ln_b, w_o, vres):
    b, s, d = h.shape
    xx = jnp.pad(h, ((0, 0), (1, 0), (0, 0)))[:, :-1] - h
    xr, xw, xk, xv, xa, xg = (h + xx * mu[j] for j in range(6))
    r = xr @ w_rkv[0]
    k = xk @ w_rkv[1]
    v = xv @ w_rkv[2]
    w_log = -jax.nn.softplus(-(w0 + jnp.tanh(xw @ w1) @ w2)) - 0.5
    decay = jnp.exp(-jnp.exp(w_log.astype(F32)))
    a = jax.nn.sigmoid(a0 + (xa @ a1) @ a2)
    if vres is not None:
        v0, v1, v2 = vres
        v = v + (v_first - v) * jax.nn.sigmoid(v0 + (xv @ v1) @ v2)
    g = jax.nn.sigmoid(xg @ g1) @ g2

    def heads(t):
        return t.reshape(b, s, RW_HEADS, RW_HEAD).astype(F32)

    kk = heads(k * k_k)
    kk = kk / jnp.maximum(jnp.sqrt(jnp.sum(kk * kk, axis=-1, keepdims=True)), 1e-12)
    k = k * (1 + (a - 1) * k_a)
    rh, kh, vh, ah = heads(r), heads(k), heads(v), heads(a)
    y = _rwkv7_scan(rh, heads(decay), kh, vh, -kk, kk * ah)
    yn = _head_layer_norm(y, RW_GN_EPS).reshape(b, s, d) * ln_w.astype(F32) + ln_b.astype(F32)
    bonus = (jnp.sum(rh * kh * r_k.astype(F32), axis=-1, keepdims=True) * vh).reshape(b, s, d)
    out = ((yn + bonus).astype(h.dtype) * g) @ w_o
    return out, v


def _retention_scan(q, k, v, log_gamma):
    bsz, _, nh, dk = q.shape
    dv = v.shape[-1]
    idx = jnp.arange(CHUNK, dtype=F32)
    diff = idx[:, None] - idx[None, :]
    causal = diff >= 0
    intra = jnp.where(causal, jnp.exp(log_gamma[:, None, None] * jnp.where(causal, diff, 0.0)), 0.0)
    xi = jnp.exp(log_gamma[:, None] * (idx + 1.0)).T[None, :, :, None]
    zeta = jnp.exp(log_gamma[:, None] * (CHUNK - 1.0 - idx)).T[None, :, :, None]
    g_chunk = jnp.exp(log_gamma * CHUNK)[None, :, None, None]

    def step(r_st, inp):
        qc, kc, vc = inp
        sc = jnp.einsum('bihd,bjhd->bhij', qc, kc) * intra
        inner = jnp.einsum('bhij,bjhe->bihe', sc, vc)
        cross = jnp.einsum('bihd,bhde->bihe', qc, r_st) * xi
        r_st = g_chunk * r_st + jnp.einsum('bjhd,bjhe->bhde', kc * zeta, vc)
        return r_st, inner + cross

    xs = (_to_chunks(q), _to_chunks(k), _to_chunks(v))
    _, ys = lax.scan(step, jnp.zeros((bsz, nh, dk, dv), F32), xs)
    return _from_chunks(ys)


def _retention_mix(h, cos, sin, w_in, gn_w, w_out):
    b, s, d = h.shape
    proj = h @ w_in
    q = proj[..., :d].reshape(b, s, RET_HEADS, RET_DK)
    k = proj[..., d:2 * d].reshape(b, s, RET_HEADS, RET_DK)
    v = proj[..., 2 * d:4 * d].reshape(b, s, RET_HEADS, RET_DV)
    g = proj[..., 4 * d:]
    q = _rotary(q, cos, sin)
    k = _rotary(k, cos, sin) * (RET_DK ** -0.5)
    log_gamma = jnp.log1p(-jnp.exp2(-5.0 - jnp.arange(RET_HEADS, dtype=F32)))
    y = _retention_scan(q.astype(F32), k.astype(F32), v.astype(F32), log_gamma)
    yn = _head_layer_norm(y, NORM_EPS).reshape(b, s, 2 * d) * gn_w.astype(F32)
    return (jax.nn.silu(g) * yn.astype(h.dtype)) @ w_out


def _mlstm_scan(q, k, v, i_pre, log_f):
    bsz, _, nh, dk = q.shape
    dv = v.shape[-1]
    causal = jnp.tril(jnp.ones((CHUNK, CHUNK), dtype=bool))

    def step(carry, inp):
        c_st, n_st, m_st = carry
        qc, kc, vc, ic, fc = inp
        bcum = jnp.moveaxis(jnp.cumsum(fc, axis=1), 1, 2)
        ig = jnp.moveaxis(ic, 1, 2)
        log_d = jnp.where(causal, bcum[..., :, None] - bcum[..., None, :] + ig[..., None, :], -jnp.inf)
        log_inter = bcum + m_st[..., None]
        m_t = jnp.maximum(log_inter, jnp.max(log_d, axis=-1))
        dmat = jnp.exp(log_d - m_t[..., None])
        w_inter = jnp.exp(log_inter - m_t)
        sc = jnp.einsum('blhd,bshd->bhls', qc, kc) * dmat
        num = jnp.einsum('bhls,bshe->bhle', sc, vc) + w_inter[..., None] * jnp.einsum('blhd,bhde->bhle', qc, c_st)
        dot = jnp.sum(sc, axis=-1) + w_inter * jnp.einsum('blhd,bhd->bhl', qc, n_st)
        hc = num / jnp.maximum(jnp.abs(dot), jnp.exp(-m_t))[..., None]
        m_new = m_t[..., -1]
        w_s = jnp.exp(bcum[..., -1:] - bcum + ig - m_new[..., None])
        dec = jnp.exp(bcum[..., -1] + m_st - m_new)
        c_st = dec[..., None, None] * c_st + jnp.einsum('bhs,bshd,bshe->bhde', w_s, kc, vc)
        n_st = dec[..., None] * n_st + jnp.einsum('bhs,bshd->bhd', w_s, kc)
        return (c_st, n_st, m_new), jnp.moveaxis(hc, 1, 2)

    init = (jnp.zeros((bsz, nh, dk, dv), F32), jnp.zeros((bsz, nh, dk), F32), jnp.zeros((bsz, nh), F32))
    xs = tuple(_to_chunks(t) for t in (q, k, v, i_pre, log_f))
    _, ys = lax.scan(step, init, xs)
    return _from_chunks(ys)


def _mlstm_mix(h, w_in, b_if, hn_w, w_out):
    b, s, d = h.shape
    dq = ML_HEADS * ML_DQK
    proj = h @ w_in
    q = proj[..., :dq].reshape(b, s, ML_HEADS, ML_DQK)
    k = proj[..., dq:2 * dq].reshape(b, s, ML_HEADS, ML_DQK) * (ML_DQK ** -0.5)
    v = proj[..., 2 * dq:2 * dq + d].reshape(b, s, ML_HEADS, ML_DV)
    o = proj[..., 2 * dq + d:2 * dq + 2 * d]
    gates = proj[..., 2 * dq + 2 * d:].astype(F32) + b_if.astype(F32)
    gates = ML_GATE_CAP * jnp.tanh(gates / ML_GATE_CAP)
    i_pre = gates[..., :ML_HEADS]
    log_f = jax.nn.log_sigmoid(gates[..., ML_HEADS:])
    y = _mlstm_scan(q.astype(F32), k.astype(F32), v.astype(F32), i_pre, log_f)
    yf = y * lax.rsqrt(jnp.mean(y * y, axis=-1, keepdims=True) + NORM_EPS)
    yn = yf.reshape(b, s, d) * hn_w.astype(F32)
    return (jax.nn.sigmoid(o) * yn.astype(h.dtype)) @ w_out


def _swiglu(h, w1, w3, w2):
    return (jax.nn.silu(h @ w1) * (h @ w3)) @ w2


def setup_inputs(seed: int = 0) -> dict:
    key = jax.random.key(seed)
    keys = iter(jax.random.split(key, 40))
    D = D_MODEL

    def nrm(shape, scale):
        return jax.random.normal(next(keys), shape, F32) * scale

    def gain(shape):
        return 1.0 + nrm(shape, 0.02)

    x = nrm((BATCH, SEQ, D), 1.0)
    offs = CHUNK * jax.random.randint(next(keys), (BATCH,), 0, 64, dtype=jnp.int32)
    positions = offs[:, None] + jnp.arange(SEQ, dtype=jnp.int32)[None, :]
    nr, nv = N_RWKV, max(N_RWKV - 1, 1)
    inp = {
        'x': x,
        'positions': positions,
        'norm_mix': gain((DEPTH, D)),
        'norm_ffn': gain((DEPTH, D)),
        'norm_final': gain((D,)),
        'rw_mu': jax.random.uniform(next(keys), (nr, 6, D), F32),
        'rw_w_rkv': nrm((nr, 3, D, D), D ** -0.5),
        'rw_w0': jax.random.uniform(next(keys), (nr, D), F32, -6.0, -1.0),
        'rw_w1': nrm((nr, D, RW_DECAY_LORA), D ** -0.5),
        'rw_w2': nrm((nr, RW_DECAY_LORA, D), 0.1 * RW_DECAY_LORA ** -0.5),
        'rw_a0': nrm((nr, D), 0.1),
        'rw_a1': nrm((nr, D, RW_AAA_LORA), D ** -0.5),
        'rw_a2': nrm((nr, RW_AAA_LORA, D), 0.5 * RW_AAA_LORA ** -0.5),
        'rw_g1': nrm((nr, D, RW_GATE_LORA), D ** -0.5),
        'rw_g2': nrm((nr, RW_GATE_LORA, D), RW_GATE_LORA ** -0.5),
        'rw_k_k': 0.85 + nrm((nr, D), 0.02),
        'rw_k_a': gain((nr, D)),
        'rw_r_k': nrm((nr, RW_HEADS, RW_HEAD), 0.1),
        'rw_ln_w': gain((nr, D)),
        'rw_ln_b': nrm((nr, D), 0.02),
        'rw_w_o': nrm((nr, D, D), D ** -0.5),
        'rw_v0': gain((nv, D)),
        'rw_v1': nrm((nv, D, RW_MV_LORA), D ** -0.5),
        'rw_v2': nrm((nv, RW_MV_LORA, D), 0.5 * RW_MV_LORA ** -0.5),
        'ret_w_in': nrm((N_RET, D, 6 * D), D ** -0.5),
        'ret_gn_w': gain((N_RET, 2 * D)),
        'ret_w_out': nrm((N_RET, 2 * D, D), (2 * D) ** -0.5),
        'ml_w_in': nrm((N_MLSTM, D, 3 * D + 2 * ML_HEADS), D ** -0.5),
        'ml_b_if': jnp.concatenate([nrm((N_MLSTM, ML_HEADS), 0.5), 3.0 + nrm((N_MLSTM, ML_HEADS), 0.5)], axis=-1),
        'ml_hn_w': gain((N_MLSTM, D)),
        'ml_w_out': nrm((N_MLSTM, D, D), D ** -0.5),
        'ffn_w1': nrm((DEPTH, D, D_FF), D ** -0.5),
        'ffn_w3': nrm((DEPTH, D, D_FF), D ** -0.5),
        'ffn_w2': nrm((DEPTH, D_FF, D), D_FF ** -0.5),
    }
    return inp


def reference(x, positions, norm_mix, norm_ffn, norm_final, rw_mu, rw_w_rkv, rw_w0, rw_w1, rw_w2,
              rw_a0, rw_a1, rw_a2, rw_g1, rw_g2, rw_k_k, rw_k_a, rw_r_k, rw_ln_w, rw_ln_b, rw_w_o,
              rw_v0, rw_v1, rw_v2, ret_w_in, ret_gn_w, ret_w_out, ml_w_in, ml_b_if, ml_hn_w, ml_w_out,
              ffn_w1, ffn_w3, ffn_w2):
    cos, sin = _rotary_tables(positions)
    v_first = None
    for i in range(DEPTH):
        kind, j = i % N_MIXERS, i // N_MIXERS
        h = _rms_norm(x, norm_mix[i])
        if kind == 0:
            vres = None if j == 0 else (rw_v0[j - 1], rw_v1[j - 1], rw_v2[j - 1])
            y, v = _rwkv7_mix(h, v_first, rw_mu[j], rw_w_rkv[j], rw_w0[j], rw_w1[j], rw_w2[j],
                              rw_a0[j], rw_a1[j], rw_a2[j], rw_g1[j], rw_g2[j], rw_k_k[j], rw_k_a[j],
                              rw_r_k[j], rw_ln_w[j], rw_ln_b[j], rw_w_o[j], vres)
            if j == 0:
                v_first = v
        elif kind == 1:
            y = _retention_mix(h, cos, sin, ret_w_in[j], ret_gn_w[j], ret_w_out[j])
        else:
            y = _mlstm_mix(h, ml_w_in[j], ml_b_if[j], ml_hn_w[j], ml_w_out[j])
        x = x + y
        x = x + _swiglu(_rms_norm(x, norm_ffn[i]), ffn_w1[i], ffn_w3[i], ffn_w2[i])
    return _rms_norm(x, norm_final)
```

```python
import functools

import jax
import jax.numpy as jnp
from jax import lax
from jax.experimental import pallas as pl
from jax.experimental.pallas import tpu as pltpu

F32 = jnp.float32
BF16 = jnp.bfloat16

NORM_EPS = 1e-6
RW_HEAD = 64
RW_GN_EPS = 64e-5
RW_CHUNK = 64
RET_HEADS = 8
ROPE_BASE = 10000.0
ML_HEADS = 4
ML_GATE_CAP = 15.0
LANES = 128
VMEM_LIMIT = 56 * 1024 * 1024


def _params(*sem):
    return pltpu.CompilerParams(dimension_semantics=sem, vmem_limit_bytes=VMEM_LIMIT)


def _mm(a, b):
    return jnp.dot(a.astype(BF16), b.astype(BF16), preferred_element_type=F32)


def _mm_nt(a, b):
    return lax.dot_general(a.astype(BF16), b.astype(BF16), (((1,), (1,)), ((), ())),
                           preferred_element_type=F32)


def _mm_tn(a, b):
    return lax.dot_general(a.astype(BF16), b.astype(BF16), (((0,), (0,)), ((), ())),
                           preferred_element_type=F32)


def _rms(x, g):
    return x * lax.rsqrt(jnp.mean(x * x, axis=-1, keepdims=True) + NORM_EPS) * g


def _sigmoid(x):
    return 1.0 / (1.0 + jnp.exp(-x))


def _softplus(x):
    return jnp.maximum(x, 0.0) + jnp.log(1.0 + jnp.exp(-jnp.abs(x)))


def _cumsum_rows(x):
    n = x.shape[0]
    row = lax.broadcasted_iota(jnp.int32, x.shape, 0)
    s = 1
    while s < n:
        x = x + jnp.where(row >= s, pltpu.roll(x, s, 0), 0.0)
        s *= 2
    return x


def _seg64_sum(x):
    r = lax.broadcasted_iota(jnp.int32, (LANES, LANES), 0) // RW_HEAD
    c = lax.broadcasted_iota(jnp.int32, (LANES, LANES), 1) // RW_HEAD
    bd = jnp.where(r == c, 1.0, 0.0).astype(BF16)
    hi = x.astype(BF16)
    lo = (x - hi.astype(F32)).astype(BF16)
    return (jnp.dot(hi, bd, preferred_element_type=F32) + jnp.dot(lo, bd, preferred_element_type=F32))


def _ffn_kernel(x_ref, g_ref, w1_ref, w3_ref, w2_ref, gf_ref, o_ref, h_s, acc_s, *, final_norm):
    j = pl.program_id(1)

    @pl.when(j == 0)
    def _():
        x = x_ref[...]
        h_s[...] = _rms(x, g_ref[...]).astype(BF16)
        acc_s[...] = x

    h = h_s[...]
    a = jnp.dot(h, w1_ref[...], preferred_element_type=F32)
    b = jnp.dot(h, w3_ref[...], preferred_element_type=F32)
    act = (a * _sigmoid(a) * b).astype(BF16)
    acc_s[...] += jnp.dot(act, w2_ref[...], preferred_element_type=F32)

    @pl.when(j == pl.num_programs(1) - 1)
    def _():
        y = acc_s[...]
        if final_norm:
            y = _rms(y, gf_ref[...])
        o_ref[...] = y


def _ffn(x, g, w1, w3, w2, gf, final_norm, tm=512, tf=512):
    t, d = x.shape
    ff = w1.shape[1]
    tf = min(tf, ff)
    return pl.pallas_call(
        functools.partial(_ffn_kernel, final_norm=final_norm),
        grid=(t // tm, ff // tf),
        in_specs=[
            pl.BlockSpec((tm, d), lambda i, j: (i, 0)),
            pl.BlockSpec((1, d), lambda i, j: (0, 0)),
            pl.BlockSpec((d, tf), lambda i, j: (0, j)),
            pl.BlockSpec((d, tf), lambda i, j: (0, j)),
            pl.BlockSpec((tf, d), lambda i, j: (j, 0)),
            pl.BlockSpec((1, d), lambda i, j: (0, 0)),
        ],
        out_specs=pl.BlockSpec((tm, d), lambda i, j: (i, 0)),
        out_shape=jax.ShapeDtypeStruct((t, d), F32),
        scratch_shapes=[pltpu.VMEM((tm, d), BF16), pltpu.VMEM((tm, d), F32)],
        compiler_params=_params("parallel", "arbitrary"),
        name="ffn",
    )(x, g, w1, w3, w2, gf)


def _outproj_kernel(z_ref, w_ref, x_ref, o_ref):
    o_ref[...] = x_ref[...] + jnp.dot(z_ref[...], w_ref[...], preferred_element_type=F32)


def _outproj(z, w, x, tm=512, tn=1024):
    t, k = z.shape
    d = w.shape[1]
    return pl.pallas_call(
        _outproj_kernel,
        grid=(t // tm, d // tn),
        in_specs=[
            pl.BlockSpec((tm, k), lambda i, j: (i, 0)),
            pl.BlockSpec((k, tn), lambda i, j: (0, j)),
            pl.BlockSpec((tm, tn), lambda i, j: (i, j)),
        ],
        out_specs=pl.BlockSpec((tm, tn), lambda i, j: (i, j)),
        out_shape=jax.ShapeDtypeStruct((t, d), F32),
        compiler_params=_params("parallel", "arbitrary"),
        name="outproj",
    )(z, w, x)


def _normproj_kernel(x_ref, g_ref, w_ref, o_ref, h_s):
    @pl.when(pl.program_id(1) == 0)
    def _():
        h_s[...] = _rms(x_ref[...], g_ref[...]).astype(BF16)

    o_ref[...] = jnp.dot(h_s[...], w_ref[...], preferred_element_type=F32)


def _normproj(x, g, w, tn, tm=512):
    t, d = x.shape
    n = w.shape[1]
    return pl.pallas_call(
        _normproj_kernel,
        grid=(t // tm, n // tn),
        in_specs=[
            pl.BlockSpec((tm, d), lambda i, j: (i, 0)),
            pl.BlockSpec((1, d), lambda i, j: (0, 0)),
            pl.BlockSpec((d, tn), lambda i, j: (0, j)),
        ],
        out_specs=pl.BlockSpec((tm, tn), lambda i, j: (i, j)),
        out_shape=jax.ShapeDtypeStruct((t, n), F32),
        scratch_shapes=[pltpu.VMEM((tm, d), BF16)],
        compiler_params=_params("parallel", "arbitrary"),
        name="normproj",
    )(x, g, w)


RW_LORA_W = 0
RW_LORA_A = 128
RW_LORA_V = 256
RW_LORA_G = 384
RW_LORA_END = 640
RW_PROLOGUE_ROWS = 128


def _rwproj_kernel(*refs, tiles_per_seq, has_vres):
    if has_vres:
        (x_ref, xp_ref, g_ref, mu_ref, wrkv_ref, lin_ref, lout_ref, vec_ref, vf_ref,
         r_o, lw_o, k_o, v_o, a_o, b_o, g_o, xr_s, xk_s, xv_s, l1_s) = refs
    else:
        (x_ref, xp_ref, g_ref, mu_ref, wrkv_ref, lin_ref, lout_ref, vec_ref,
         r_o, lw_o, k_o, v_o, a_o, b_o, g_o, xr_s, xk_s, xv_s, l1_s) = refs
        vf_ref = None
    i = pl.program_id(0)

    @pl.when(pl.program_id(1) == 0)
    def _():
        g = g_ref[...]
        mu = mu_ref[...]
        sub = RW_PROLOGUE_ROWS
        for s in range(x_ref.shape[0] // sub):
            rows = slice(s * sub, (s + 1) * sub)
            h = _rms(x_ref[rows, :], g)
            if s == 0:
                hp = _rms(xp_ref[7:8, :], g)
                hp = jnp.where(i % tiles_per_seq == 0, 0.0, hp)
            else:
                hp = _rms(x_ref[s * sub - 8:s * sub, :], g)[7:8, :]
            row = lax.broadcasted_iota(jnp.int32, h.shape, 0)
            xx = jnp.where(row == 0, hp, pltpu.roll(h, 1, 0)) - h
            xr_s[rows, :] = (h + xx * mu[0:1]).astype(BF16)
            xk_s[rows, :] = (h + xx * mu[2:3]).astype(BF16)
            xv = (h + xx * mu[3:4]).astype(BF16)
            xv_s[rows, :] = xv
            xw = h + xx * mu[1:2]
            xa = h + xx * mu[4:5]
            xg = h + xx * mu[5:6]
            l1_s[rows, RW_LORA_W:RW_LORA_A] = jnp.tanh(_mm(xw, lin_ref[:, RW_LORA_W:RW_LORA_A])).astype(BF16)
            l1_s[rows, RW_LORA_A:RW_LORA_V] = _mm(xa, lin_ref[:, RW_LORA_A:RW_LORA_V]).astype(BF16)
            l1_s[rows, RW_LORA_V:RW_LORA_G] = _mm(xv, lin_ref[:, RW_LORA_V:RW_LORA_G]).astype(BF16)
            l1_s[rows, RW_LORA_G:RW_LORA_END] = _sigmoid(
                _mm(xg, lin_ref[:, RW_LORA_G:RW_LORA_END])).astype(BF16)

    r = jnp.dot(xr_s[...], wrkv_ref[0], preferred_element_type=F32)
    k = jnp.dot(xk_s[...], wrkv_ref[1], preferred_element_type=F32)
    v = jnp.dot(xv_s[...], wrkv_ref[2], preferred_element_type=F32)
    vec = vec_ref[...]
    w0, a0, v0, k_k, k_a = (vec[n:n + 1] for n in range(5))

    def lora2(lo, hi):
        return jnp.dot(l1_s[:, lo:hi], lout_ref[lo:hi, :], preferred_element_type=F32)

    w_log = -_softplus(-(w0 + lora2(RW_LORA_W, RW_LORA_A))) - 0.5
    lw_o[...] = -jnp.exp(w_log)
    a = _sigmoid(a0 + lora2(RW_LORA_A, RW_LORA_V))
    if has_vres:
        v = v + (vf_ref[...] - v) * _sigmoid(v0 + lora2(RW_LORA_V, RW_LORA_G))
    g_o[...] = lora2(RW_LORA_G, RW_LORA_END)
    kk = k * k_k
    tn = kk.shape[1]
    ss = jnp.concatenate([_seg64_sum(jnp.square(kk[:, c:c + LANES])) for c in range(0, tn, LANES)], axis=1)
    kk = kk / jnp.maximum(jnp.sqrt(ss), 1e-12)
    r_o[...] = r
    k_o[...] = k * (1.0 + (a - 1.0) * k_a)
    v_o[...] = v
    a_o[...] = -kk
    b_o[...] = kk * a


def _rwproj(x, g, mu, wrkv, lin, lout, vecs, vfirst, seq, tm=512, tn=512):
    t, d = x.shape
    has_vres = vfirst is not None
    rows8 = tm // 8
    in_specs = [
        pl.BlockSpec((tm, d), lambda i, j: (i, 0)),
        pl.BlockSpec((8, d), lambda i, j: (jnp.maximum(i * rows8 - 1, 0), 0)),
        pl.BlockSpec((1, d), lambda i, j: (0, 0)),
        pl.BlockSpec((6, d), lambda i, j: (0, 0)),
        pl.BlockSpec((3, d, tn), lambda i, j: (0, 0, j)),
        pl.BlockSpec((d, RW_LORA_END), lambda i, j: (0, 0)),
        pl.BlockSpec((RW_LORA_END, tn), lambda i, j: (0, j)),
        pl.BlockSpec((8, tn), lambda i, j: (0, j)),
    ]
    args = [x, x, g, mu, wrkv, lin, lout, vecs]
    if has_vres:
        in_specs.append(pl.BlockSpec((tm, tn), lambda i, j: (i, j)))
        args.append(vfirst)
    out = jax.ShapeDtypeStruct((t, d), F32)
    return pl.pallas_call(
        functools.partial(_rwproj_kernel, tiles_per_seq=seq // tm, has_vres=has_vres),
        grid=(t // tm, d // tn),
        in_specs=in_specs,
        out_specs=[pl.BlockSpec((tm, tn), lambda i, j: (i, j))] * 7,
        out_shape=[out] * 7,
        scratch_shapes=[pltpu.VMEM((tm, d), BF16)] * 3 + [pltpu.VMEM((tm, RW_LORA_END), BF16)],
        compiler_params=_params("parallel", "arbitrary"),
        name="rwproj",
    )(*args)


def _unit_lower_inverse(n):
    size = n.shape[0]
    eye = jnp.where(lax.broadcasted_iota(jnp.int32, n.shape, 0) == lax.broadcasted_iota(jnp.int32, n.shape, 1),
                    1.0, 0.0)
    t = eye + n
    p = n
    span = 2
    while span < size:
        p = _mm(p, p)
        t = t + _mm(t, p)
        span *= 2
    return t


def _rwscan_kernel(r_ref, lw_ref, k_ref, v_ref, a_ref, b_ref, g_ref, vec_ref, o_ref, s_ref, *, chunks):
    cl = RW_CHUNK

    @pl.when(pl.program_id(2) == 0)
    def _():
        s_ref[...] = jnp.zeros_like(s_ref)

    row = lax.broadcasted_iota(jnp.int32, (cl, cl), 0)
    col = lax.broadcasted_iota(jnp.int32, (cl, cl), 1)
    strict = row > col
    incl = row >= col
    vec = vec_ref[...]
    r_k, ln_w, ln_b = vec[0:1], vec[1:2], vec[2:3]

    for c in range(chunks):
        rows = slice(c * cl, (c + 1) * cl)
        r, lw, k, v = r_ref[rows, :], lw_ref[rows, :], k_ref[rows, :], v_ref[rows, :]
        a, b = a_ref[rows, :], b_ref[rows, :]
        cum = _cumsum_rows(lw)
        last = cum[cl - 1:cl, :]
        e_neg = jnp.exp(-cum)
        e_tail = jnp.exp(last - cum)
        rt = r * jnp.exp(cum)
        at = a * jnp.exp(cum - lw)
        bt = b * e_neg
        kt = k * e_neg
        bl = b * e_tail
        kl = k * e_tail
        p_last = jnp.exp(last)
        ys = []
        for h in range(LANES // RW_HEAD):
            hs = slice(h * RW_HEAD, (h + 1) * RW_HEAD)
            gram = _mm_nt(jnp.concatenate([at[:, hs], rt[:, hs]], axis=0),
                          jnp.concatenate([bt[:, hs], kt[:, hs]], axis=0))
            n_ab = jnp.where(strict, gram[:cl, :cl], 0.0)
            a_ak = jnp.where(strict, gram[:cl, cl:], 0.0)
            a_rb = jnp.where(incl, gram[cl:, :cl], 0.0)
            a_rk = jnp.where(incl, gram[cl:, cl:], 0.0)
            tinv = _unit_lower_inverse(n_ab)
            vh = v[:, hs]
            tz = _mm(tinv, jnp.concatenate([at[:, hs], _mm(a_ak, vh)], axis=1))
            ap, vp = tz[:, :RW_HEAD], tz[:, RW_HEAD:]
            q = _mm(a_rb, tz)
            rp = rt[:, hs] + q[:, :RW_HEAD]
            yp = q[:, RW_HEAD:] + _mm(a_rk, vh)
            m_t = _mm_tn(ap, bl[:, hs])
            c_t = _mm_tn(jnp.concatenate([vp, vh], axis=0),
                         jnp.concatenate([bl[:, hs], kl[:, hs]], axis=0))
            s = s_ref[h]
            ys.append(_mm_nt(rp, s) + yp)
            s_ref[h] = p_last[:, hs] * s + _mm(s, m_t) + c_t
        y = jnp.concatenate(ys, axis=1)
        mean = _seg64_sum(y) * (1.0 / RW_HEAD)
        yc = y - mean
        var = _seg64_sum(yc * yc) * (1.0 / RW_HEAD)
        yn = yc * lax.rsqrt(var + RW_GN_EPS) * ln_w + ln_b
        bonus = _seg64_sum(r * k * r_k) * v
        o_ref[rows, :] = ((yn + bonus) * g_ref[rows, :]).astype(BF16)


def _rwscan(r, lw, k, v, a, b, g, vecs, batch, seq, lt=256):
    t, d = r.shape
    nt = seq // lt
    blk = pl.BlockSpec((lt, LANES), lambda bi, hp, ti: (bi * nt + ti, hp))
    return pl.pallas_call(
        functools.partial(_rwscan_kernel, chunks=lt // RW_CHUNK),
        grid=(batch, d // LANES, nt),
        in_specs=[blk] * 7 + [pl.BlockSpec((8, LANES), lambda bi, hp, ti: (0, hp))],
        out_specs=blk,
        out_shape=jax.ShapeDtypeStruct((t, d), BF16),
        scratch_shapes=[pltpu.VMEM((LANES // RW_HEAD, RW_HEAD, RW_HEAD), F32)],
        compiler_params=_params("parallel", "parallel", "arbitrary"),
        name="rwscan",
    )(r, lw, k, v, a, b, g, vecs)


def _rope_kernel(pos_ref, inv_ref, cos_ref, sin_ref):
    ang = pos_ref[...].astype(F32) * inv_ref[...]
    cos_ref[...] = jnp.cos(ang)
    sin_ref[...] = jnp.sin(ang)


def _rope_tables(pos, inv_freq, tm=512):
    t = pos.shape[0]
    half = inv_freq.shape[1]
    out = jax.ShapeDtypeStruct((t, half), F32)
    return pl.pallas_call(
        _rope_kernel,
        grid=(t // tm,),
        in_specs=[pl.BlockSpec((tm, 1), lambda i: (i, 0)), pl.BlockSpec((1, half), lambda i: (0, 0))],
        out_specs=[pl.BlockSpec((tm, half), lambda i: (i, 0))] * 2,
        out_shape=[out, out],
        compiler_params=_params("parallel"),
        name="rope",
    )(pos, inv_freq)


def _retscan_kernel(q_ref, k_ref, v_ref, gate_ref, cos_ref, sin_ref, lg_ref, gn_ref, o_ref, st_ref):
    lc, dk = q_ref.shape

    @pl.when(pl.program_id(2) == 0)
    def _():
        st_ref[...] = jnp.zeros_like(st_ref)

    cos, sin = cos_ref[...], sin_ref[...]
    half = dk // 2

    def rot(t):
        t1, t2 = t[:, :half], t[:, half:]
        return jnp.concatenate([t1 * cos - t2 * sin, t1 * sin + t2 * cos], axis=1)

    q = rot(q_ref[...])
    k = rot(k_ref[...]) * (dk ** -0.5)
    v = v_ref[...]
    lg = lg_ref[0][:, 0:1]
    ic = lax.broadcasted_iota(jnp.int32, (lc, 1), 0).astype(F32)
    ir = lax.broadcasted_iota(jnp.int32, (1, lc), 1).astype(F32)
    diff = ic - ir
    causal = diff >= 0
    intra = jnp.where(causal, jnp.exp(lg * jnp.where(causal, diff, 0.0)), 0.0)
    xi = jnp.exp(lg * (ic + 1.0))
    zeta = jnp.exp(lg * (lc - 1.0 - ic))
    st = st_ref[...]
    sc = _mm_nt(q, k) * intra
    y = _mm(sc, v) + _mm(q, st) * xi
    st_ref[...] = jnp.exp(lg * lc) * st + _mm_tn(k * zeta, v)
    mean = jnp.mean(y, axis=-1, keepdims=True)
    yc = y - mean
    yn = yc * lax.rsqrt(jnp.mean(yc * yc, axis=-1, keepdims=True) + NORM_EPS) * gn_ref[...]
    gate = gate_ref[...]
    o_ref[...] = (gate * _sigmoid(gate) * yn).astype(BF16)


def _retscan(proj, cos, sin, lg, gn_w, batch, seq, d, lc=256):
    t = proj.shape[0]
    nc = seq // lc
    dk, dv = d // RET_HEADS, 2 * d // RET_HEADS
    tok = lambda bi, h, ci: bi * nc + ci
    return pl.pallas_call(
        _retscan_kernel,
        grid=(batch, RET_HEADS, nc),
        in_specs=[
            pl.BlockSpec((lc, dk), lambda bi, h, ci: (tok(bi, h, ci), h)),
            pl.BlockSpec((lc, dk), lambda bi, h, ci: (tok(bi, h, ci), RET_HEADS + h)),
            pl.BlockSpec((lc, dv), lambda bi, h, ci: (tok(bi, h, ci), RET_HEADS + h)),
            pl.BlockSpec((lc, dv), lambda bi, h, ci: (tok(bi, h, ci), 2 * RET_HEADS + h)),
            pl.BlockSpec((lc, dk // 2), lambda bi, h, ci: (tok(bi, h, ci), 0)),
            pl.BlockSpec((lc, dk // 2), lambda bi, h, ci: (tok(bi, h, ci), 0)),
            pl.BlockSpec((1, 1, LANES), lambda bi, h, ci: (h, 0, 0)),
            pl.BlockSpec((1, dv), lambda bi, h, ci: (0, h)),
        ],
        out_specs=pl.BlockSpec((lc, dv), lambda bi, h, ci: (tok(bi, h, ci), h)),
        out_shape=jax.ShapeDtypeStruct((t, 2 * d), BF16),
        scratch_shapes=[pltpu.VMEM((dk, dv), F32)],
        compiler_params=_params("parallel", "parallel", "arbitrary"),
        name="retscan",
    )(proj, proj, proj, proj, cos, sin, lg, gn_w)


def _mlscan_kernel(q_ref, k_ref, v_ref, og_ref, gt_ref, bif_ref, hn_ref, o_ref, c_ref, m_ref):
    lc = q_ref.shape[0]
    dqk = q_ref.shape[1] // ML_HEADS
    dv = v_ref.shape[1] // ML_HEADS

    @pl.when(pl.program_id(1) == 0)
    def _():
        c_ref[...] = jnp.zeros_like(c_ref)
        m_ref[...] = jnp.zeros_like(m_ref)

    gt = gt_ref[...] + bif_ref[...]
    gt = ML_GATE_CAP * jnp.tanh(gt * (1.0 / ML_GATE_CAP))
    lane = lax.broadcasted_iota(jnp.int32, gt.shape, 1)
    is_f = (lane >= ML_HEADS) & (lane < 2 * ML_HEADS)
    log_f = jnp.where(is_f, -_softplus(-gt), 0.0)
    bcum = pltpu.roll(_cumsum_rows(log_f), LANES - ML_HEADS, 1)
    rj_t = (bcum - gt).T
    ir = lax.broadcasted_iota(jnp.int32, (lc, lc), 0)
    ic = lax.broadcasted_iota(jnp.int32, (lc, lc), 1)
    causal = ir >= ic
    ones_col = jnp.where(lax.broadcasted_iota(jnp.int32, (lc, LANES), 1) == 0, 1.0, 0.0)
    for h in range(ML_HEADS):
        q = q_ref[:, h * dqk:(h + 1) * dqk]
        k = k_ref[:, h * dqk:(h + 1) * dqk] * (dqk ** -0.5)
        vext = jnp.concatenate([v_ref[:, h * dv:(h + 1) * dv], ones_col], axis=1)
        b_col = bcum[:, h:h + 1]
        i_col = gt[:, h:h + 1]
        m_st = m_ref[h:h + 1, 0:1]
        log_d = jnp.where(causal, b_col - rj_t[h:h + 1, :], -jnp.inf)
        log_inter = b_col + m_st
        m_t = jnp.maximum(log_inter, jnp.max(log_d, axis=-1, keepdims=True))
        dmat = jnp.exp(log_d - m_t)
        w_inter = jnp.exp(log_inter - m_t)
        sc = _mm_nt(q, k) * dmat
        c_st = c_ref[h]
        num = _mm(sc, vext) + w_inter * _mm(q, c_st)
        dot = num[:, dv:dv + 1]
        hc = num[:, :dv] / jnp.maximum(jnp.abs(dot), jnp.exp(-m_t))
        m_new = m_t[lc - 1:lc, :]
        b_last = b_col[lc - 1:lc, :]
        w_s = jnp.exp(b_last - b_col + i_col - m_new)
        dec = jnp.exp(b_last + m_st - m_new)
        c_ref[h] = dec * c_st + _mm_tn(k * w_s, vext)
        m_ref[h:h + 1, :] = jnp.broadcast_to(m_new, (1, LANES))
        yf = hc * lax.rsqrt(jnp.mean(hc * hc, axis=-1, keepdims=True) + NORM_EPS)
        og = og_ref[:, h * dv:(h + 1) * dv]
        o_ref[:, h * dv:(h + 1) * dv] = (_sigmoid(og) * (yf * hn_ref[:, h * dv:(h + 1) * dv])).astype(BF16)


def _mlscan(proj, bif, hn_w, batch, seq, d, lc=256):
    t = proj.shape[0]
    nc = seq // lc
    dq = d // 2
    tok = lambda bi, ci: bi * nc + ci
    return pl.pallas_call(
        _mlscan_kernel,
        grid=(batch, nc),
        in_specs=[
            pl.BlockSpec((lc, dq), lambda bi, ci: (tok(bi, ci), 0)),
            pl.BlockSpec((lc, dq), lambda bi, ci: (tok(bi, ci), 1)),
            pl.BlockSpec((lc, d), lambda bi, ci: (tok(bi, ci), 1)),
            pl.BlockSpec((lc, d), lambda bi, ci: (tok(bi, ci), 2)),
            pl.BlockSpec((lc, LANES), lambda bi, ci: (tok(bi, ci), 3 * d // LANES)),
            pl.BlockSpec((1, LANES), lambda bi, ci: (0, 0)),
            pl.BlockSpec((1, d), lambda bi, ci: (0, 0)),
        ],
        out_specs=pl.BlockSpec((lc, d), lambda bi, ci: (tok(bi, ci), 0)),
        out_shape=jax.ShapeDtypeStruct((t, d), BF16),
        scratch_shapes=[pltpu.VMEM((ML_HEADS, dq // ML_HEADS, d // ML_HEADS + LANES), F32),
                        pltpu.VMEM((8, LANES), F32)],
        compiler_params=_params("parallel", "arbitrary"),
        name="mlscan",
    )(proj, proj, proj, proj, proj, bif, hn_w)


def _pad_cols(w, n):
    return jnp.pad(w, ((0, 0), (0, n - w.shape[1])))


def _pad_rows(w, n):
    return jnp.pad(w, ((0, n - w.shape[0]), (0, 0)))


def kernel(x, positions, norm_mix, norm_ffn, norm_final, rw_mu, rw_w_rkv, rw_w0, rw_w1, rw_w2, rw_a0, rw_a1, rw_a2, rw_g1, rw_g2, rw_k_k, rw_k_a, rw_r_k, rw_ln_w, rw_ln_b, rw_w_o, rw_v0, rw_v1, rw_v2, ret_w_in, ret_gn_w, ret_w_out, ml_w_in, ml_b_if, ml_hn_w, ml_w_out, ffn_w1, ffn_w3, ffn_w2):
    batch, seq, d = x.shape
    t = batch * seq
    depth = norm_mix.shape[0]
    xf = x.reshape(t, d)
    row = lambda vec: vec.reshape(1, -1).astype(F32)

    dk = d // RET_HEADS
    inv_freq = (1.0 / (ROPE_BASE ** jnp.linspace(0.0, 1.0, dk // 2, dtype=F32))).reshape(1, -1)
    cos, sin = _rope_tables(positions.reshape(t, 1), inv_freq)
    log_gamma = jnp.log1p(-jnp.exp2(-5.0 - jnp.arange(RET_HEADS, dtype=F32)))
    lg = jnp.broadcast_to(log_gamma[:, None, None], (RET_HEADS, 1, LANES))

    v_first = None
    for i in range(depth):
        kind, j = i % 3, i // 3
        g_mix = row(norm_mix[i])
        if kind == 0:
            has_vres = j > 0
            zeros_v1 = jnp.zeros((d, RW_LORA_G - RW_LORA_V), F32)
            zeros_v2 = jnp.zeros((RW_LORA_G - RW_LORA_V, d), F32)
            lin = jnp.concatenate([
                _pad_cols(rw_w1[j], RW_LORA_A - RW_LORA_W), _pad_cols(rw_a1[j], RW_LORA_V - RW_LORA_A),
                _pad_cols(rw_v1[j - 1], RW_LORA_G - RW_LORA_V) if has_vres else zeros_v1,
                rw_g1[j]], axis=1).astype(BF16)
            lout = jnp.concatenate([
                _pad_rows(rw_w2[j], RW_LORA_A - RW_LORA_W), _pad_rows(rw_a2[j], RW_LORA_V - RW_LORA_A),
                _pad_rows(rw_v2[j - 1], RW_LORA_G - RW_LORA_V) if has_vres else zeros_v2,
                rw_g2[j]], axis=0).astype(BF16)
            v0 = rw_v0[j - 1] if has_vres else jnp.zeros((d,), F32)
            zero = jnp.zeros((d,), F32)
            vecs = jnp.stack([rw_w0[j], rw_a0[j], v0, rw_k_k[j], rw_k_a[j], zero, zero, zero]).astype(F32)
            r, lw, k, v, a, b, g = _rwproj(xf, g_mix, rw_mu[j].astype(F32), rw_w_rkv[j].astype(BF16), lin, lout,
                                           vecs, v_first if has_vres else None, seq)
            if j == 0:
                v_first = v
            svecs = jnp.stack([rw_r_k[j].reshape(-1), rw_ln_w[j], rw_ln_b[j], zero, zero, zero, zero, zero]).astype(F32)
            z = _rwscan(r, lw, k, v, a, b, g, svecs, batch, seq)
            xf = _outproj(z, rw_w_o[j].astype(BF16), xf)
        elif kind == 1:
            proj = _normproj(xf, g_mix, ret_w_in[j].astype(BF16), tn=1024)
            z = _retscan(proj, cos, sin, lg, row(ret_gn_w[j]), batch, seq, d)
            xf = _outproj(z, ret_w_out[j].astype(BF16), xf)
        else:
            n_in = ml_w_in.shape[2]
            n_pad = 3 * d + LANES
            w_in = _pad_cols(ml_w_in[j], n_pad).astype(BF16)
            proj = _normproj(xf, g_mix, w_in, tn=n_pad // 7)
            bif = _pad_cols(ml_b_if[j].reshape(1, -1).astype(F32), LANES)
            z = _mlscan(proj, bif, row(ml_hn_w[j]), batch, seq, d)
            xf = _outproj(z, ml_w_out[j].astype(BF16), xf)
        xf = _ffn(xf, row(norm_ffn[i]), ffn_w1[i].astype(BF16), ffn_w3[i].astype(BF16), ffn_w2[i].astype(BF16),
                  row(norm_final), final_norm=(i == depth - 1))
    return xf.reshape(batch, seq, d)
```

```python
import functools

import jax
import jax.numpy as jnp
from jax import lax
from jax.experimental import pallas as pl
from jax.experimental.pallas import tpu as pltpu

F32 = jnp.float32
BF16 = jnp.bfloat16

NORM_EPS = 1e-6
RW_HEAD = 64
RW_GN_EPS = 64e-5
RW_CHUNK = 64
RET_HEADS = 8
ROPE_BASE = 10000.0
ML_HEADS = 4
ML_GATE_CAP = 15.0
LANES = 128
VMEM_LIMIT = 56 * 1024 * 1024


def _params(*sem):
    return pltpu.CompilerParams(dimension_semantics=sem, vmem_limit_bytes=VMEM_LIMIT)


def _mm(a, b):
    return jnp.dot(a.astype(BF16), b.astype(BF16), preferred_element_type=F32)


def _mm_nt(a, b):
    return lax.dot_general(a.astype(BF16), b.astype(BF16), (((1,), (1,)), ((), ())),
                           preferred_element_type=F32)


def _mm_tn(a, b):
    return lax.dot_general(a.astype(BF16), b.astype(BF16), (((0,), (0,)), ((), ())),
                           preferred_element_type=F32)


def _rms(x, g):
    return x * lax.rsqrt(jnp.mean(x * x, axis=-1, keepdims=True) + NORM_EPS) * g


def _sigmoid(x):
    return 1.0 / (1.0 + jnp.exp(-x))


def _softplus(x):
    return jnp.maximum(x, 0.0) + jnp.log(1.0 + jnp.exp(-jnp.abs(x)))


def _cumsum_rows(x):
    n = x.shape[0]
    row = lax.broadcasted_iota(jnp.int32, x.shape, 0)
    s = 1
    while s < n:
        x = x + jnp.where(row >= s, pltpu.roll(x, s, 0), 0.0)
        s *= 2
    return x


def _seg64_sum(x):
    r = lax.broadcasted_iota(jnp.int32, (LANES, LANES), 0) // RW_HEAD
    c = lax.broadcasted_iota(jnp.int32, (LANES, LANES), 1) // RW_HEAD
    bd = jnp.where(r == c, 1.0, 0.0).astype(BF16)
    hi = x.astype(BF16)
    lo = (x - hi.astype(F32)).astype(BF16)
    return (jnp.dot(hi, bd, preferred_element_type=F32) + jnp.dot(lo, bd, preferred_element_type=F32))


def _ffn_kernel(x_ref, g_ref, w1_ref, w3_ref, w2_ref, gf_ref, o_ref, h_s, acc_s, *, final_norm):
    j = pl.program_id(1)

    @pl.when(j == 0)
    def _():
        x = x_ref[...]
        h_s[...] = _rms(x, g_ref[...]).astype(BF16)
        acc_s[...] = x

    h = h_s[...]
    a = jnp.dot(h, w1_ref[...], preferred_element_type=F32)
    b = jnp.dot(h, w3_ref[...], preferred_element_type=F32)
    act = (a * _sigmoid(a) * b).astype(BF16)
    acc_s[...] += jnp.dot(act, w2_ref[...], preferred_element_type=F32)

    @pl.when(j == pl.num_programs(1) - 1)
    def _():
        y = acc_s[...]
        if final_norm:
            y = _rms(y, gf_ref[...])
        o_ref[...] = y


def _ffn(x, g, w1, w3, w2, gf, final_norm, tm=512, tf=512):
    t, d = x.shape
    ff = w1.shape[1]
    tf = min(tf, ff)
    return pl.pallas_call(
        functools.partial(_ffn_kernel, final_norm=final_norm),
        grid=(t // tm, ff // tf),
        in_specs=[
            pl.BlockSpec((tm, d), lambda i, j: (i, 0)),
            pl.BlockSpec((1, d), lambda i, j: (0, 0)),
            pl.BlockSpec((d, tf), lambda i, j: (0, j)),
            pl.BlockSpec((d, tf), lambda i, j: (0, j)),
            pl.BlockSpec((tf, d), lambda i, j: (j, 0)),
            pl.BlockSpec((1, d), lambda i, j: (0, 0)),
        ],
        out_specs=pl.BlockSpec((tm, d), lambda i, j: (i, 0)),
        out_shape=jax.ShapeDtypeStruct((t, d), F32),
        scratch_shapes=[pltpu.VMEM((tm, d), BF16), pltpu.VMEM((tm, d), F32)],
        compiler_params=_params("parallel", "arbitrary"),
        name="ffn",
    )(x, g, w1, w3, w2, gf)


def _outproj_kernel(z_ref, w_ref, x_ref, o_ref):
    o_ref[...] = x_ref[...] + jnp.dot(z_ref[...], w_ref[...], preferred_element_type=F32)


def _outproj(z, w, x, tm=512, tn=1024):
    t, k = z.shape
    d = w.shape[1]
    return pl.pallas_call(
        _outproj_kernel,
        grid=(t // tm, d // tn),
        in_specs=[
            pl.BlockSpec((tm, k), lambda i, j: (i, 0)),
            pl.BlockSpec((k, tn), lambda i, j: (0, j)),
            pl.BlockSpec((tm, tn), lambda i, j: (i, j)),
        ],
        out_specs=pl.BlockSpec((tm, tn), lambda i, j: (i, j)),
        out_shape=jax.ShapeDtypeStruct((t, d), F32),
        compiler_params=_params("parallel", "arbitrary"),
        name="outproj",
    )(z, w, x)


def _normproj_kernel(x_ref, g_ref, w_ref, o_ref, h_s):
    @pl.when(pl.program_id(1) == 0)
    def _():
        h_s[...] = _rms(x_ref[...], g_ref[...]).astype(BF16)

    o_ref[...] = jnp.dot(h_s[...], w_ref[...], preferred_element_type=F32)


def _normproj(x, g, w, tn, tm=512):
    t, d = x.shape
    n = w.shape[1]
    return pl.pallas_call(
        _normproj_kernel,
        grid=(t // tm, n // tn),
        in_specs=[
            pl.BlockSpec((tm, d), lambda i, j: (i, 0)),
            pl.BlockSpec((1, d), lambda i, j: (0, 0)),
            pl.BlockSpec((d, tn), lambda i, j: (0, j)),
        ],
        out_specs=pl.BlockSpec((tm, tn), lambda i, j: (i, j)),
        out_shape=jax.ShapeDtypeStruct((t, n), F32),
        scratch_shapes=[pltpu.VMEM((tm, d), BF16)],
        compiler_params=_params("parallel", "arbitrary"),
        name="normproj",
    )(x, g, w)


RW_LORA_W = 0
RW_LORA_A = 128
RW_LORA_V = 256
RW_LORA_G = 384
RW_LORA_END = 640
RW_PROLOGUE_ROWS = 128


def _rwproj_kernel(*refs, tiles_per_seq, has_vres):
    if has_vres:
        (x_ref, xp_ref, g_ref, mu_ref, wrkv_ref, lin_ref, lout_ref, vec_ref, vf_ref,
         r_o, lw_o, k_o, v_o, a_o, b_o, g_o, xr_s, xk_s, xv_s, l1_s) = refs
    else:
        (x_ref, xp_ref, g_ref, mu_ref, wrkv_ref, lin_ref, lout_ref, vec_ref,
         r_o, lw_o, k_o, v_o, a_o, b_o, g_o, xr_s, xk_s, xv_s, l1_s) = refs
        vf_ref = None
    i = pl.program_id(0)

    @pl.when(pl.program_id(1) == 0)
    def _():
        g = g_ref[...]
        mu = mu_ref[...]
        sub = RW_PROLOGUE_ROWS
        for s in range(x_ref.shape[0] // sub):
            rows = slice(s * sub, (s + 1) * sub)
            h = _rms(x_ref[rows, :], g)
            if s == 0:
                hp = _rms(xp_ref[7:8, :], g)
                hp = jnp.where(i % tiles_per_seq == 0, 0.0, hp)
            else:
                hp = _rms(x_ref[s * sub - 8:s * sub, :], g)[7:8, :]
            row = lax.broadcasted_iota(jnp.int32, h.shape, 0)
            xx = jnp.where(row == 0, hp, pltpu.roll(h, 1, 0)) - h
            xr_s[rows, :] = (h + xx * mu[0:1]).astype(BF16)
            xk_s[rows, :] = (h + xx * mu[2:3]).astype(BF16)
            xv = (h + xx * mu[3:4]).astype(BF16)
            xv_s[rows, :] = xv
            xw = h + xx * mu[1:2]
            xa = h + xx * mu[4:5]
            xg = h + xx * mu[5:6]
            l1_s[rows, RW_LORA_W:RW_LORA_A] = jnp.tanh(_mm(xw, lin_ref[:, RW_LORA_W:RW_LORA_A])).astype(BF16)
            l1_s[rows, RW_LORA_A:RW_LORA_V] = _mm(xa, lin_ref[:, RW_LORA_A:RW_LORA_V]).astype(BF16)
            l1_s[rows, RW_LORA_V:RW_LORA_G] = _mm(xv, lin_ref[:, RW_LORA_V:RW_LORA_G]).astype(BF16)
            l1_s[rows, RW_LORA_G:RW_LORA_END] = _sigmoid(
                _mm(xg, lin_ref[:, RW_LORA_G:RW_LORA_END])).astype(BF16)

    r = jnp.dot(xr_s[...], wrkv_ref[0], preferred_element_type=F32)
    k = jnp.dot(xk_s[...], wrkv_ref[1], preferred_element_type=F32)
    v = jnp.dot(xv_s[...], wrkv_ref[2], preferred_element_type=F32)
    vec = vec_ref[...]
    w0, a0, v0, k_k, k_a = (vec[n:n + 1] for n in range(5))

    def lora2(lo, hi):
        return jnp.dot(l1_s[:, lo:hi], lout_ref[lo:hi, :], preferred_element_type=F32)

    w_log = -_softplus(-(w0 + lora2(RW_LORA_W, RW_LORA_A))) - 0.5
    lw_o[...] = -jnp.exp(w_log)
    a = _sigmoid(a0 + lora2(RW_LORA_A, RW_LORA_V))
    if has_vres:
        v = v + (vf_ref[...] - v) * _sigmoid(v0 + lora2(RW_LORA_V, RW_LORA_G))
    g_o[...] = lora2(RW_LORA_G, RW_LORA_END)
    kk = k * k_k
    tn = kk.shape[1]
    ss = jnp.concatenate([_seg64_sum(jnp.square(kk[:, c:c + LANES])) for c in range(0, tn, LANES)], axis=1)
    kk = kk / jnp.maximum(jnp.sqrt(ss), 1e-12)
    r_o[...] = r
    k_o[...] = k * (1.0 + (a - 1.0) * k_a)
    v_o[...] = v
    a_o[...] = -kk
    b_o[...] = kk * a


def _rwproj(x, g, mu, wrkv, lin, lout, vecs, vfirst, seq, tm=512, tn=512):
    t, d = x.shape
    has_vres = vfirst is not None
    rows8 = tm // 8
    in_specs = [
        pl.BlockSpec((tm, d), lambda i, j: (i, 0)),
        pl.BlockSpec((8, d), lambda i, j: (jnp.maximum(i * rows8 - 1, 0), 0)),
        pl.BlockSpec((1, d), lambda i, j: (0, 0)),
        pl.BlockSpec((6, d), lambda i, j: (0, 0)),
        pl.BlockSpec((3, d, tn), lambda i, j: (0, 0, j)),
        pl.BlockSpec((d, RW_LORA_END), lambda i, j: (0, 0)),
        pl.BlockSpec((RW_LORA_END, tn), lambda i, j: (0, j)),
        pl.BlockSpec((8, tn), lambda i, j: (0, j)),
    ]
    args = [x, x, g, mu, wrkv, lin, lout, vecs]
    if has_vres:
        in_specs.append(pl.BlockSpec((tm, tn), lambda i, j: (i, j)))
        args.append(vfirst)
    out = jax.ShapeDtypeStruct((t, d), F32)
    return pl.pallas_call(
        functools.partial(_rwproj_kernel, tiles_per_seq=seq // tm, has_vres=has_vres),
        grid=(t // tm, d // tn),
        in_specs=in_specs,
        out_specs=[pl.BlockSpec((tm, tn), lambda i, j: (i, j))] * 7,
        out_shape=[out] * 7,
        scratch_shapes=[pltpu.VMEM((tm, d), BF16)] * 3 + [pltpu.VMEM((tm, RW_LORA_END), BF16)],
        compiler_params=_params("parallel", "arbitrary"),
        name="rwproj",
    )(*args)


def _rwscan_kernel(r_ref, lw_ref, k_ref, v_ref, a_ref, b_ref, g_ref, vec_ref, o_ref, s_ref, *, chunks):
    cl = RW_CHUNK
    pair = 2 * cl
    lt = chunks * cl

    @pl.when(pl.program_id(2) == 0)
    def _():
        s_ref[...] = jnp.zeros_like(s_ref)

    head0 = lax.broadcasted_iota(jnp.int32, (cl, LANES), 1) < RW_HEAD

    def stack(x):
        return jnp.concatenate([jnp.where(head0, x, 0.0), jnp.where(head0, 0.0, x)], axis=0).astype(BF16)

    ri = lax.broadcasted_iota(jnp.int32, (pair, pair), 0)
    ci = lax.broadcasted_iota(jnp.int32, (pair, pair), 1)
    same = (ri // cl) == (ci // cl)
    strict = same & ((ri % cl) > (ci % cl))
    incl = same & ((ri % cl) >= (ci % cl))
    eye = jnp.where(ri == ci, 1.0, 0.0)

    lw = lw_ref[...]
    rowc = lax.broadcasted_iota(jnp.int32, (lt, LANES), 0) % cl
    cum = lw
    s = 1
    while s < cl:
        cum = cum + jnp.where(rowc >= s, pltpu.roll(cum, s, 0), 0.0)
        s *= 2
    r, k, v = r_ref[...], k_ref[...], v_ref[...]
    a, b = a_ref[...], b_ref[...]
    e_neg = jnp.exp(-cum)
    rt = r * jnp.exp(cum)
    at = a * jnp.exp(cum - lw)
    bt = b * e_neg
    kt = k * e_neg

    cs = range(chunks)
    rows = [slice(c * cl, (c + 1) * cl) for c in cs]
    last = [cum[(c + 1) * cl - 1:(c + 1) * cl, :] for c in cs]
    tail = [jnp.exp(last[c] - cum[rows[c]]) for c in cs]
    at_s = [stack(at[rows[c]]) for c in cs]
    rt_s = [stack(rt[rows[c]]) for c in cs]
    v_s = [stack(v[rows[c]]) for c in cs]
    bl_s = [stack(b[rows[c]] * tail[c]) for c in cs]
    kl_s = [stack(k[rows[c]] * tail[c]) for c in cs]
    gram = [_mm_nt(jnp.concatenate([at_s[c], rt_s[c]], axis=0),
                   jnp.concatenate([stack(bt[rows[c]]), stack(kt[rows[c]])], axis=0)) for c in cs]
    n_ab = [jnp.where(strict, gram[c][:pair, :pair], 0.0) for c in cs]
    a_ak = [jnp.where(strict, gram[c][:pair, pair:], 0.0).astype(BF16) for c in cs]
    a_r = [jnp.concatenate([jnp.where(incl, gram[c][pair:, :pair], 0.0),
                            jnp.where(incl, gram[c][pair:, pair:], 0.0)], axis=1).astype(BF16) for c in cs]
    x1 = [_mm(a_ak[c], v_s[c]) for c in cs]
    tinv = [eye + n_ab[c] for c in cs]
    pw = [n_ab[c].astype(BF16) for c in cs]
    pw = [_mm(pw[c], pw[c]).astype(BF16) for c in cs]
    span = 4
    while span < cl:
        both = [_mm(jnp.concatenate([pw[c], tinv[c].astype(BF16)], axis=0), pw[c]) for c in cs]
        pw = [both[c][:pair].astype(BF16) for c in cs]
        tinv = [tinv[c] + both[c][pair:] for c in cs]
        span *= 2
    tinv = [tinv[c] + _mm(tinv[c], pw[c]) for c in cs]
    tz = [_mm(tinv[c], jnp.concatenate([at_s[c], x1[c].astype(BF16)], axis=1)) for c in cs]
    zeros = jnp.zeros((pair, LANES), BF16)
    q = [_mm(a_r[c], jnp.concatenate([tz[c].astype(BF16), jnp.concatenate([zeros, v_s[c]], axis=1)], axis=0))
         for c in cs]
    rp = [rt_s[c].astype(F32) + q[c][:, :LANES] for c in cs]
    mc = [_mm_tn(tz[c], bl_s[c]) for c in cs]
    vk = [_mm_tn(v_s[c], kl_s[c]) for c in cs]

    st = s_ref[...]
    ys = []
    for c in cs:
        y_s = _mm_nt(rp[c], st) + q[c][:, LANES:]
        ys.append(y_s[:cl] + y_s[cl:])
        st = jnp.exp(last[c]) * st + _mm(st, mc[c][:LANES]) + (mc[c][LANES:] + vk[c])
    s_ref[...] = st

    vec = vec_ref[...]
    r_k, ln_w, ln_b = vec[0:1], vec[1:2], vec[2:3]
    y = jnp.concatenate(ys, axis=0)
    mean = _seg64_sum(y) * (1.0 / RW_HEAD)
    yc = y - mean
    var = _seg64_sum(yc * yc) * (1.0 / RW_HEAD)
    yn = yc * lax.rsqrt(var + RW_GN_EPS) * ln_w + ln_b
    bonus = _seg64_sum(r * k * r_k) * v
    o_ref[...] = ((yn + bonus) * g_ref[...]).astype(BF16)


def _rwscan(r, lw, k, v, a, b, g, vecs, batch, seq, lt=512):
    t, d = r.shape
    nt = seq // lt
    blk = pl.BlockSpec((lt, LANES), lambda bi, hp, ti: (bi * nt + ti, hp))
    return pl.pallas_call(
        functools.partial(_rwscan_kernel, chunks=lt // RW_CHUNK),
        grid=(batch, d // LANES, nt),
        in_specs=[blk] * 7 + [pl.BlockSpec((8, LANES), lambda bi, hp, ti: (0, hp))],
        out_specs=blk,
        out_shape=jax.ShapeDtypeStruct((t, d), BF16),
        scratch_shapes=[pltpu.VMEM((LANES, LANES), F32)],
        compiler_params=_params("parallel", "parallel", "arbitrary"),
        name="rwscan",
    )(r, lw, k, v, a, b, g, vecs)


def _rope_kernel(pos_ref, inv_ref, cos_ref, sin_ref):
    ang = pos_ref[...].astype(F32) * inv_ref[...]
    cos_ref[...] = jnp.cos(ang)
    sin_ref[...] = jnp.sin(ang)


def _rope_tables(pos, inv_freq, tm=512):
    t = pos.shape[0]
    half = inv_freq.shape[1]
    out = jax.ShapeDtypeStruct((t, half), F32)
    return pl.pallas_call(
        _rope_kernel,
        grid=(t // tm,),
        in_specs=[pl.BlockSpec((tm, 1), lambda i: (i, 0)), pl.BlockSpec((1, half), lambda i: (0, 0))],
        out_specs=[pl.BlockSpec((tm, half), lambda i: (i, 0))] * 2,
        out_shape=[out, out],
        compiler_params=_params("parallel"),
        name="rope",
    )(pos, inv_freq)


def _retscan_kernel(q_ref, k_ref, v_ref, gate_ref, cos_ref, sin_ref, lg_ref, gn_ref, o_ref, st_ref):
    lc, dk = q_ref.shape

    @pl.when(pl.program_id(2) == 0)
    def _():
        st_ref[...] = jnp.zeros_like(st_ref)

    cos, sin = cos_ref[...], sin_ref[...]
    half = dk // 2

    def rot(t):
        t1, t2 = t[:, :half], t[:, half:]
        return jnp.concatenate([t1 * cos - t2 * sin, t1 * sin + t2 * cos], axis=1)

    q = rot(q_ref[...])
    k = rot(k_ref[...]) * (dk ** -0.5)
    v = v_ref[...]
    lg = lg_ref[0][:, 0:1]
    ic = lax.broadcasted_iota(jnp.int32, (lc, 1), 0).astype(F32)
    ir = lax.broadcasted_iota(jnp.int32, (1, lc), 1).astype(F32)
    diff = ic - ir
    causal = diff >= 0
    intra = jnp.where(causal, jnp.exp(lg * jnp.where(causal, diff, 0.0)), 0.0)
    xi = jnp.exp(lg * (ic + 1.0))
    zeta = jnp.exp(lg * (lc - 1.0 - ic))
    st = st_ref[...]
    sc = _mm_nt(q, k) * intra
    y = _mm(sc, v) + _mm(q, st) * xi
    st_ref[...] = jnp.exp(lg * lc) * st + _mm_tn(k * zeta, v)
    mean = jnp.mean(y, axis=-1, keepdims=True)
    yc = y - mean
    yn = yc * lax.rsqrt(jnp.mean(yc * yc, axis=-1, keepdims=True) + NORM_EPS) * gn_ref[...]
    gate = gate_ref[...]
    o_ref[...] = (gate * _sigmoid(gate) * yn).astype(BF16)


def _retscan(proj, cos, sin, lg, gn_w, batch, seq, d, lc=256):
    t = proj.shape[0]
    nc = seq // lc
    dk, dv = d // RET_HEADS, 2 * d // RET_HEADS
    tok = lambda bi, h, ci: bi * nc + ci
    return pl.pallas_call(
        _retscan_kernel,
        grid=(batch, RET_HEADS, nc),
        in_specs=[
            pl.BlockSpec((lc, dk), lambda bi, h, ci: (tok(bi, h, ci), h)),
            pl.BlockSpec((lc, dk), lambda bi, h, ci: (tok(bi, h, ci), RET_HEADS + h)),
            pl.BlockSpec((lc, dv), lambda bi, h, ci: (tok(bi, h, ci), RET_HEADS + h)),
            pl.BlockSpec((lc, dv), lambda bi, h, ci: (tok(bi, h, ci), 2 * RET_HEADS + h)),
            pl.BlockSpec((lc, dk // 2), lambda bi, h, ci: (tok(bi, h, ci), 0)),
            pl.BlockSpec((lc, dk // 2), lambda bi, h, ci: (tok(bi, h, ci), 0)),
            pl.BlockSpec((1, 1, LANES), lambda bi, h, ci: (h, 0, 0)),
            pl.BlockSpec((1, dv), lambda bi, h, ci: (0, h)),
        ],
        out_specs=pl.BlockSpec((lc, dv), lambda bi, h, ci: (tok(bi, h, ci), h)),
        out_shape=jax.ShapeDtypeStruct((t, 2 * d), BF16),
        scratch_shapes=[pltpu.VMEM((dk, dv), F32)],
        compiler_params=_params("parallel", "parallel", "arbitrary"),
        name="retscan",
    )(proj, proj, proj, proj, cos, sin, lg, gn_w)


def _mlscan_kernel(q_ref, k_ref, v_ref, og_ref, gt_ref, bif_ref, hn_ref, o_ref, c_ref, m_ref):
    lc = q_ref.shape[0]
    dqk = q_ref.shape[1] // ML_HEADS
    dv = v_ref.shape[1] // ML_HEADS

    @pl.when(pl.program_id(1) == 0)
    def _():
        c_ref[...] = jnp.zeros_like(c_ref)
        m_ref[...] = jnp.zeros_like(m_ref)

    gt = gt_ref[...] + bif_ref[...]
    gt = ML_GATE_CAP * jnp.tanh(gt * (1.0 / ML_GATE_CAP))
    lane = lax.broadcasted_iota(jnp.int32, gt.shape, 1)
    is_f = (lane >= ML_HEADS) & (lane < 2 * ML_HEADS)
    log_f = jnp.where(is_f, -_softplus(-gt), 0.0)
    bcum = pltpu.roll(_cumsum_rows(log_f), LANES - ML_HEADS, 1)
    rj_t = (bcum - gt).T
    ir = lax.broadcasted_iota(jnp.int32, (lc, lc), 0)
    ic = lax.broadcasted_iota(jnp.int32, (lc, lc), 1)
    causal = ir >= ic
    ones_col = jnp.where(lax.broadcasted_iota(jnp.int32, (lc, LANES), 1) == 0, 1.0, 0.0)
    for h in range(ML_HEADS):
        q = q_ref[:, h * dqk:(h + 1) * dqk]
        k = k_ref[:, h * dqk:(h + 1) * dqk] * (dqk ** -0.5)
        vext = jnp.concatenate([v_ref[:, h * dv:(h + 1) * dv], ones_col], axis=1)
        b_col = bcum[:, h:h + 1]
        i_col = gt[:, h:h + 1]
        m_st = m_ref[h:h + 1, 0:1]
        log_d = jnp.where(causal, b_col - rj_t[h:h + 1, :], -jnp.inf)
        log_inter = b_col + m_st
        m_t = jnp.maximum(log_inter, jnp.max(log_d, axis=-1, keepdims=True))
        dmat = jnp.exp(log_d - m_t)
        w_inter = jnp.exp(log_inter - m_t)
        sc = _mm_nt(q, k) * dmat
        c_st = c_ref[h]
        num = _mm(sc, vext) + w_inter * _mm(q, c_st)
        dot = num[:, dv:dv + 1]
        hc = num[:, :dv] / jnp.maximum(jnp.abs(dot), jnp.exp(-m_t))
        m_new = m_t[lc - 1:lc, :]
        b_last = b_col[lc - 1:lc, :]
        w_s = jnp.exp(b_last - b_col + i_col - m_new)
        dec = jnp.exp(b_last + m_st - m_new)
        c_ref[h] = dec * c_st + _mm_tn(k * w_s, vext)
        m_ref[h:h + 1, :] = jnp.broadcast_to(m_new, (1, LANES))
        yf = hc * lax.rsqrt(jnp.mean(hc * hc, axis=-1, keepdims=True) + NORM_EPS)
        og = og_ref[:, h * dv:(h + 1) * dv]
        o_ref[:, h * dv:(h + 1) * dv] = (_sigmoid(og) * (yf * hn_ref[:, h * dv:(h + 1) * dv])).astype(BF16)


def _mlscan(proj, bif, hn_w, batch, seq, d, lc=256):
    t = proj.shape[0]
    nc = seq // lc
    dq = d // 2
    tok = lambda bi, ci: bi * nc + ci
    return pl.pallas_call(
        _mlscan_kernel,
        grid=(batch, nc),
        in_specs=[
            pl.BlockSpec((lc, dq), lambda bi, ci: (tok(bi, ci), 0)),
            pl.BlockSpec((lc, dq), lambda bi, ci: (tok(bi, ci), 1)),
            pl.BlockSpec((lc, d), lambda bi, ci: (tok(bi, ci), 1)),
            pl.BlockSpec((lc, d), lambda bi, ci: (tok(bi, ci), 2)),
            pl.BlockSpec((lc, LANES), lambda bi, ci: (tok(bi, ci), 3 * d // LANES)),
            pl.BlockSpec((1, LANES), lambda bi, ci: (0, 0)),
            pl.BlockSpec((1, d), lambda bi, ci: (0, 0)),
        ],
        out_specs=pl.BlockSpec((lc, d), lambda bi, ci: (tok(bi, ci), 0)),
        out_shape=jax.ShapeDtypeStruct((t, d), BF16),
        scratch_shapes=[pltpu.VMEM((ML_HEADS, dq // ML_HEADS, d // ML_HEADS + LANES), F32),
                        pltpu.VMEM((8, LANES), F32)],
        compiler_params=_params("parallel", "arbitrary"),
        name="mlscan",
    )(proj, proj, proj, proj, proj, bif, hn_w)


def _pad_cols(w, n):
    return jnp.pad(w, ((0, 0), (0, n - w.shape[1])))


def _pad_rows(w, n):
    return jnp.pad(w, ((0, n - w.shape[0]), (0, 0)))


def kernel(x, positions, norm_mix, norm_ffn, norm_final, rw_mu, rw_w_rkv, rw_w0, rw_w1, rw_w2, rw_a0, rw_a1, rw_a2, rw_g1, rw_g2, rw_k_k, rw_k_a, rw_r_k, rw_ln_w, rw_ln_b, rw_w_o, rw_v0, rw_v1, rw_v2, ret_w_in, ret_gn_w, ret_w_out, ml_w_in, ml_b_if, ml_hn_w, ml_w_out, ffn_w1, ffn_w3, ffn_w2):
    batch, seq, d = x.shape
    t = batch * seq
    depth = norm_mix.shape[0]
    xf = x.reshape(t, d)
    row = lambda vec: vec.reshape(1, -1).astype(F32)

    dk = d // RET_HEADS
    inv_freq = (1.0 / (ROPE_BASE ** jnp.linspace(0.0, 1.0, dk // 2, dtype=F32))).reshape(1, -1)
    cos, sin = _rope_tables(positions.reshape(t, 1), inv_freq)
    log_gamma = jnp.log1p(-jnp.exp2(-5.0 - jnp.arange(RET_HEADS, dtype=F32)))
    lg = jnp.broadcast_to(log_gamma[:, None, None], (RET_HEADS, 1, LANES))

    v_first = None
    for i in range(depth):
        kind, j = i % 3, i // 3
        g_mix = row(norm_mix[i])
        if kind == 0:
            has_vres = j > 0
            zeros_v1 = jnp.zeros((d, RW_LORA_G - RW_LORA_V), F32)
            zeros_v2 = jnp.zeros((RW_LORA_G - RW_LORA_V, d), F32)
            lin = jnp.concatenate([
                _pad_cols(rw_w1[j], RW_LORA_A - RW_LORA_W), _pad_cols(rw_a1[j], RW_LORA_V - RW_LORA_A),
                _pad_cols(rw_v1[j - 1], RW_LORA_G - RW_LORA_V) if has_vres else zeros_v1,
                rw_g1[j]], axis=1).astype(BF16)
            lout = jnp.concatenate([
                _pad_rows(rw_w2[j], RW_LORA_A - RW_LORA_W), _pad_rows(rw_a2[j], RW_LORA_V - RW_LORA_A),
                _pad_rows(rw_v2[j - 1], RW_LORA_G - RW_LORA_V) if has_vres else zeros_v2,
                rw_g2[j]], axis=0).astype(BF16)
            v0 = rw_v0[j - 1] if has_vres else jnp.zeros((d,), F32)
            zero = jnp.zeros((d,), F32)
            vecs = jnp.stack([rw_w0[j], rw_a0[j], v0, rw_k_k[j], rw_k_a[j], zero, zero, zero]).astype(F32)
            r, lw, k, v, a, b, g = _rwproj(xf, g_mix, rw_mu[j].astype(F32), rw_w_rkv[j].astype(BF16), lin, lout,
                                           vecs, v_first if has_vres else None, seq)
            if j == 0:
                v_first = v
            svecs = jnp.stack([rw_r_k[j].reshape(-1), rw_ln_w[j], rw_ln_b[j], zero, zero, zero, zero, zero]).astype(F32)
            z = _rwscan(r, lw, k, v, a, b, g, svecs, batch, seq)
            xf = _outproj(z, rw_w_o[j].astype(BF16), xf)
        elif kind == 1:
            proj = _normproj(xf, g_mix, ret_w_in[j].astype(BF16), tn=1024)
            z = _retscan(proj, cos, sin, lg, row(ret_gn_w[j]), batch, seq, d)
            xf = _outproj(z, ret_w_out[j].astype(BF16), xf)
        else:
            n_in = ml_w_in.shape[2]
            n_pad = 3 * d + LANES
            w_in = _pad_cols(ml_w_in[j], n_pad).astype(BF16)
            proj = _normproj(xf, g_mix, w_in, tn=n_pad // 7)
            bif = _pad_cols(ml_b_if[j].reshape(1, -1).astype(F32), LANES)
            z = _mlscan(proj, bif, row(ml_hn_w[j]), batch, seq, d)
            xf = _outproj(z, ml_w_out[j].astype(BF16), xf)
        xf = _ffn(xf, row(norm_ffn[i]), ffn_w1[i].astype(BF16), ffn_w3[i].astype(BF16), ffn_w2[i].astype(BF16),
                  row(norm_final), final_norm=(i == depth - 1))
    return xf.reshape(batch, seq, d)
```

```python
import functools

import jax
import jax.numpy as jnp
from jax import lax
from jax.experimental import pallas as pl
from jax.experimental.pallas import tpu as pltpu

F32 = jnp.float32
BF16 = jnp.bfloat16

NORM_EPS = 1e-6
RW_HEAD = 64
RW_GN_EPS = 64e-5
RW_CHUNK = 64
RET_HEADS = 8
ROPE_BASE = 10000.0
ML_HEADS = 4
ML_GATE_CAP = 15.0
LANES = 128
VMEM_LIMIT = 56 * 1024 * 1024


def _params(*sem):
    return pltpu.CompilerParams(dimension_semantics=sem, vmem_limit_bytes=VMEM_LIMIT)


def _mm(a, b):
    return jnp.dot(a.astype(BF16), b.astype(BF16), preferred_element_type=F32)


def _mm_nt(a, b):
    return lax.dot_general(a.astype(BF16), b.astype(BF16), (((1,), (1,)), ((), ())),
                           preferred_element_type=F32)


def _mm_tn(a, b):
    return lax.dot_general(a.astype(BF16), b.astype(BF16), (((0,), (0,)), ((), ())),
                           preferred_element_type=F32)


def _rms(x, g):
    return x * lax.rsqrt(jnp.mean(x * x, axis=-1, keepdims=True) + NORM_EPS) * g


def _sigmoid(x):
    return 1.0 / (1.0 + jnp.exp(-x))


def _softplus(x):
    return jnp.maximum(x, 0.0) + jnp.log(1.0 + jnp.exp(-jnp.abs(x)))


def _cumsum_rows(x):
    n = x.shape[0]
    row = lax.broadcasted_iota(jnp.int32, x.shape, 0)
    s = 1
    while s < n:
        x = x + jnp.where(row >= s, pltpu.roll(x, s, 0), 0.0)
        s *= 2
    return x


def _seg64_sum(x):
    r = lax.broadcasted_iota(jnp.int32, (LANES, LANES), 0) // RW_HEAD
    c = lax.broadcasted_iota(jnp.int32, (LANES, LANES), 1) // RW_HEAD
    bd = jnp.where(r == c, 1.0, 0.0).astype(BF16)
    hi = x.astype(BF16)
    lo = (x - hi.astype(F32)).astype(BF16)
    return (jnp.dot(hi, bd, preferred_element_type=F32) + jnp.dot(lo, bd, preferred_element_type=F32))


def _ffn_kernel(x_ref, g_ref, w1_ref, w3_ref, w2_ref, gf_ref, o_ref, h_s, acc_s, *, final_norm):
    j = pl.program_id(1)

    @pl.when(j == 0)
    def _():
        x = x_ref[...]
        h_s[...] = _rms(x, g_ref[...]).astype(BF16)
        acc_s[...] = x

    h = h_s[...]
    a = jnp.dot(h, w1_ref[...], preferred_element_type=F32)
    b = jnp.dot(h, w3_ref[...], preferred_element_type=F32)
    act = (a * _sigmoid(a) * b).astype(BF16)
    acc_s[...] += jnp.dot(act, w2_ref[...], preferred_element_type=F32)

    @pl.when(j == pl.num_programs(1) - 1)
    def _():
        y = acc_s[...]
        if final_norm:
            y = _rms(y, gf_ref[...])
        o_ref[...] = y


def _ffn(x, g, w1, w3, w2, gf, final_norm, tm=512, tf=512):
    t, d = x.shape
    ff = w1.shape[1]
    tf = min(tf, ff)
    return pl.pallas_call(
        functools.partial(_ffn_kernel, final_norm=final_norm),
        grid=(t // tm, ff // tf),
        in_specs=[
            pl.BlockSpec((tm, d), lambda i, j: (i, 0)),
            pl.BlockSpec((1, d), lambda i, j: (0, 0)),
            pl.BlockSpec((d, tf), lambda i, j: (0, j)),
            pl.BlockSpec((d, tf), lambda i, j: (0, j)),
            pl.BlockSpec((tf, d), lambda i, j: (j, 0)),
            pl.BlockSpec((1, d), lambda i, j: (0, 0)),
        ],
        out_specs=pl.BlockSpec((tm, d), lambda i, j: (i, 0)),
        out_shape=jax.ShapeDtypeStruct((t, d), F32),
        scratch_shapes=[pltpu.VMEM((tm, d), BF16), pltpu.VMEM((tm, d), F32)],
        compiler_params=_params("parallel", "arbitrary"),
        name="ffn",
    )(x, g, w1, w3, w2, gf)


def _outproj_kernel(z_ref, w_ref, x_ref, o_ref):
    o_ref[...] = x_ref[...] + jnp.dot(z_ref[...], w_ref[...], preferred_element_type=F32)


OUTPROJ_W_BYTES = 8 * 1024 * 1024


def _outproj(z, w, x, tm=512):
    t, k = z.shape
    d = w.shape[1]
    tn = min(d, OUTPROJ_W_BYTES // (2 * k))
    return pl.pallas_call(
        _outproj_kernel,
        grid=(d // tn, t // tm),
        in_specs=[
            pl.BlockSpec((tm, k), lambda j, i: (i, 0)),
            pl.BlockSpec((k, tn), lambda j, i: (0, j)),
            pl.BlockSpec((tm, tn), lambda j, i: (i, j)),
        ],
        out_specs=pl.BlockSpec((tm, tn), lambda j, i: (i, j)),
        out_shape=jax.ShapeDtypeStruct((t, d), F32),
        compiler_params=_params("arbitrary", "arbitrary"),
        name="outproj",
    )(z, w, x)


NORM_ROWS = 256


def _normproj_kernel(x_ref, g_ref, w_ref, o_ref, *rest, side_cols):
    h_s = rest[-1]

    @pl.when(pl.program_id(1) == 0)
    def _():
        for s in range(0, x_ref.shape[0], NORM_ROWS):
            h_s[s:s + NORM_ROWS, :] = _rms(x_ref[s:s + NORM_ROWS, :], g_ref[...]).astype(BF16)

    acc = jnp.dot(h_s[...], w_ref[...], preferred_element_type=F32)
    o_ref[...] = acc.astype(o_ref.dtype)
    if side_cols:
        @pl.when(pl.program_id(1) == pl.num_programs(1) - 1)
        def _():
            rest[0][...] = acc[:, acc.shape[1] - side_cols:]


def _normproj(x, g, w, tn, side_cols=0, tm=1024):
    t, d = x.shape
    n = w.shape[1]
    out_specs = [pl.BlockSpec((tm, tn), lambda i, j: (i, j))]
    out_shape = [jax.ShapeDtypeStruct((t, n), BF16)]
    if side_cols:
        out_specs.append(pl.BlockSpec((tm, side_cols), lambda i, j: (i, 0)))
        out_shape.append(jax.ShapeDtypeStruct((t, side_cols), F32))
    return pl.pallas_call(
        functools.partial(_normproj_kernel, side_cols=side_cols),
        grid=(t // tm, n // tn),
        in_specs=[
            pl.BlockSpec((tm, d), lambda i, j: (i, 0)),
            pl.BlockSpec((1, d), lambda i, j: (0, 0)),
            pl.BlockSpec((d, tn), lambda i, j: (0, j)),
        ],
        out_specs=out_specs,
        out_shape=out_shape,
        scratch_shapes=[pltpu.VMEM((tm, d), BF16)],
        compiler_params=_params("parallel", "arbitrary"),
        name="normproj",
    )(x, g, w)


RW_LORA_W = 0
RW_LORA_A = 128
RW_LORA_V = 256
RW_LORA_G = 384
RW_LORA_END = 640
RW_PROLOGUE_ROWS = 128


def _rwproj_kernel(*refs, tiles_per_seq, has_vres):
    if has_vres:
        (x_ref, xp_ref, g_ref, mu_ref, wrkv_ref, lin_ref, lout_ref, vec_ref, vf_ref,
         r_o, lw_o, k_o, v_o, a_o, b_o, g_o, xr_s, xk_s, xv_s, l1_s) = refs
    else:
        (x_ref, xp_ref, g_ref, mu_ref, wrkv_ref, lin_ref, lout_ref, vec_ref,
         r_o, lw_o, k_o, v_o, a_o, b_o, g_o, xr_s, xk_s, xv_s, l1_s) = refs
        vf_ref = None
    i = pl.program_id(0)

    @pl.when(pl.program_id(1) == 0)
    def _():
        g = g_ref[...]
        mu = mu_ref[...]
        sub = RW_PROLOGUE_ROWS
        for s in range(x_ref.shape[0] // sub):
            rows = slice(s * sub, (s + 1) * sub)
            h = _rms(x_ref[rows, :], g)
            if s == 0:
                hp = _rms(xp_ref[7:8, :], g)
                hp = jnp.where(i % tiles_per_seq == 0, 0.0, hp)
            else:
                hp = _rms(x_ref[s * sub - 8:s * sub, :], g)[7:8, :]
            row = lax.broadcasted_iota(jnp.int32, h.shape, 0)
            xx = jnp.where(row == 0, hp, pltpu.roll(h, 1, 0)) - h
            xr_s[rows, :] = (h + xx * mu[0:1]).astype(BF16)
            xk_s[rows, :] = (h + xx * mu[2:3]).astype(BF16)
            xv = (h + xx * mu[3:4]).astype(BF16)
            xv_s[rows, :] = xv
            xw = h + xx * mu[1:2]
            xa = h + xx * mu[4:5]
            xg = h + xx * mu[5:6]
            l1_s[rows, RW_LORA_W:RW_LORA_A] = jnp.tanh(_mm(xw, lin_ref[:, RW_LORA_W:RW_LORA_A])).astype(BF16)
            l1_s[rows, RW_LORA_A:RW_LORA_V] = _mm(xa, lin_ref[:, RW_LORA_A:RW_LORA_V]).astype(BF16)
            l1_s[rows, RW_LORA_V:RW_LORA_G] = _mm(xv, lin_ref[:, RW_LORA_V:RW_LORA_G]).astype(BF16)
            l1_s[rows, RW_LORA_G:RW_LORA_END] = _sigmoid(
                _mm(xg, lin_ref[:, RW_LORA_G:RW_LORA_END])).astype(BF16)

    r = jnp.dot(xr_s[...], wrkv_ref[0], preferred_element_type=F32)
    k = jnp.dot(xk_s[...], wrkv_ref[1], preferred_element_type=F32)
    v = jnp.dot(xv_s[...], wrkv_ref[2], preferred_element_type=F32)
    vec = vec_ref[...]
    w0, a0, v0, k_k, k_a = (vec[n:n + 1] for n in range(5))

    def lora2(lo, hi):
        return jnp.dot(l1_s[:, lo:hi], lout_ref[lo:hi, :], preferred_element_type=F32)

    w_log = -_softplus(-(w0 + lora2(RW_LORA_W, RW_LORA_A))) - 0.5
    lw_o[...] = -jnp.exp(w_log)
    a = _sigmoid(a0 + lora2(RW_LORA_A, RW_LORA_V))
    if has_vres:
        v = v + (vf_ref[...].astype(F32) - v) * _sigmoid(v0 + lora2(RW_LORA_V, RW_LORA_G))
    g_o[...] = lora2(RW_LORA_G, RW_LORA_END).astype(g_o.dtype)
    kk = k * k_k
    tn = kk.shape[1]
    ss = jnp.concatenate([_seg64_sum(jnp.square(kk[:, c:c + LANES])) for c in range(0, tn, LANES)], axis=1)
    kk = kk / jnp.maximum(jnp.sqrt(ss), 1e-12)
    r_o[...] = r.astype(r_o.dtype)
    k_o[...] = (k * (1.0 + (a - 1.0) * k_a)).astype(k_o.dtype)
    v_o[...] = v.astype(v_o.dtype)
    a_o[...] = (-kk).astype(a_o.dtype)
    b_o[...] = (kk * a).astype(b_o.dtype)


def _rwproj(x, g, mu, wrkv, lin, lout, vecs, vfirst, seq, tm=512, tn=512):
    t, d = x.shape
    has_vres = vfirst is not None
    rows8 = tm // 8
    in_specs = [
        pl.BlockSpec((tm, d), lambda i, j: (i, 0)),
        pl.BlockSpec((8, d), lambda i, j: (jnp.maximum(i * rows8 - 1, 0), 0)),
        pl.BlockSpec((1, d), lambda i, j: (0, 0)),
        pl.BlockSpec((6, d), lambda i, j: (0, 0)),
        pl.BlockSpec((3, d, tn), lambda i, j: (0, 0, j)),
        pl.BlockSpec((d, RW_LORA_END), lambda i, j: (0, 0)),
        pl.BlockSpec((RW_LORA_END, tn), lambda i, j: (0, j)),
        pl.BlockSpec((8, tn), lambda i, j: (0, j)),
    ]
    args = [x, x, g, mu, wrkv, lin, lout, vecs]
    if has_vres:
        in_specs.append(pl.BlockSpec((tm, tn), lambda i, j: (i, j)))
        args.append(vfirst)
    out_dtypes = [BF16, F32, BF16, BF16, BF16, BF16, BF16]
    return pl.pallas_call(
        functools.partial(_rwproj_kernel, tiles_per_seq=seq // tm, has_vres=has_vres),
        grid=(t // tm, d // tn),
        in_specs=in_specs,
        out_specs=[pl.BlockSpec((tm, tn), lambda i, j: (i, j))] * 7,
        out_shape=[jax.ShapeDtypeStruct((t, d), dt) for dt in out_dtypes],
        scratch_shapes=[pltpu.VMEM((tm, d), BF16)] * 3 + [pltpu.VMEM((tm, RW_LORA_END), BF16)],
        compiler_params=_params("parallel", "arbitrary"),
        name="rwproj",
    )(*args)


def _rwscan_kernel(r_ref, lw_ref, k_ref, v_ref, a_ref, b_ref, g_ref, vec_ref, o_ref,
                   s_ref, rp_st, yp_st, mt_st, ct_st, pl_st, bg_st, g_st, *, chunks):
    cl = RW_CHUNK
    pair = 2 * cl
    lt = chunks * cl
    width = lw_ref.shape[1]
    pairs = width // LANES
    ti = pl.program_id(2)
    slot_w = ti % 2
    slot_r = 1 - slot_w

    @pl.when(ti == 0)
    def _():
        s_ref[...] = jnp.zeros_like(s_ref)
        for ref in (rp_st, yp_st, mt_st, ct_st, pl_st, bg_st, g_st):
            ref[1] = jnp.zeros(ref.shape[1:], ref.dtype)

    vec = vec_ref[...]
    states = [s_ref[p] for p in range(pairs)]
    ys = [[] for _ in range(pairs)]
    links_done = [0]

    def serial_link():
        j = links_done[0]
        if j >= chunks:
            return
        links_done[0] = j + 1
        for p in range(pairs):
            c = p * chunks + j
            st = states[p]
            y_s = _mm_nt(rp_st[slot_r, c], st) + yp_st[slot_r, c]
            ys[p].append(y_s[:cl] + y_s[cl:])
            states[p] = jnp.exp(pl_st[slot_r, c][0:1]) * st + _mm(st, mt_st[slot_r, c]) + ct_st[slot_r, c]

    norm = {}

    def finish_mean():
        while links_done[0] < chunks:
            serial_link()
        for p in range(pairs):
            s_ref[p] = states[p]
        norm["y"] = [jnp.concatenate(ys[p], axis=0) for p in range(pairs)]
        norm["mean"] = [_seg64_sum(y) * (1.0 / RW_HEAD) for y in norm["y"]]

    def finish_var():
        norm["yc"] = [norm["y"][p] - norm["mean"][p] for p in range(pairs)]
        norm["var"] = [_seg64_sum(yc * yc) * (1.0 / RW_HEAD) for yc in norm["yc"]]

    def finish_store():
        for p in range(pairs):
            lanes = slice(p * LANES, (p + 1) * LANES)
            yn = norm["yc"][p] * lax.rsqrt(norm["var"][p] + RW_GN_EPS) * vec[1:2, lanes] + vec[2:3, lanes]
            o_ref[:, lanes] = (yn * g_st[slot_r, :, lanes] + bg_st[slot_r, :, lanes]).astype(BF16)

    serial_link()

    head0 = lax.broadcasted_iota(jnp.int32, (cl, LANES), 1) < RW_HEAD

    def stack(x):
        return jnp.concatenate([jnp.where(head0, x, 0.0), jnp.where(head0, 0.0, x)], axis=0).astype(BF16)

    ri = lax.broadcasted_iota(jnp.int32, (pair, pair), 0)
    ci = lax.broadcasted_iota(jnp.int32, (pair, pair), 1)
    same = (ri // cl) == (ci // cl)
    strict = same & ((ri % cl) > (ci % cl))
    incl = same & ((ri % cl) >= (ci % cl))
    eye = jnp.where(ri == ci, 1.0, 0.0)

    lw = lw_ref[...]
    rowc = lax.broadcasted_iota(jnp.int32, (lt, width), 0) % cl
    cum = lw
    s = 1
    while s < cl:
        cum = cum + jnp.where(rowc >= s, pltpu.roll(cum, s, 0), 0.0)
        s *= 2
    r, k, v = r_ref[...].astype(F32), k_ref[...].astype(F32), v_ref[...].astype(F32)
    a, b = a_ref[...].astype(F32), b_ref[...].astype(F32)
    g = g_ref[...].astype(F32)
    rkr = r * k * vec[0:1]
    bonus = jnp.concatenate([_seg64_sum(rkr[:, p * LANES:(p + 1) * LANES]) for p in range(pairs)], axis=1)
    bg_st[slot_w] = bonus * v * g
    g_st[slot_w] = g
    e_neg = jnp.exp(-cum)
    rt = r * jnp.exp(cum)
    at = a * jnp.exp(cum - lw)
    bt = b * e_neg
    kt = k * e_neg

    cs = range(pairs * chunks)
    rows = [(slice((c % chunks) * cl, (c % chunks + 1) * cl),
             slice((c // chunks) * LANES, (c // chunks + 1) * LANES)) for c in cs]
    last = [cum[rows[c][0], rows[c][1]][cl - 1:cl, :] for c in cs]
    tail = [jnp.exp(last[c] - cum[rows[c]]) for c in cs]
    at_s = [stack(at[rows[c]]) for c in cs]
    rt_s = [stack(rt[rows[c]]) for c in cs]
    v_s = [stack(v[rows[c]]) for c in cs]
    bl_s = [stack(b[rows[c]] * tail[c]) for c in cs]
    kl_s = [stack(k[rows[c]] * tail[c]) for c in cs]
    gram = [_mm_nt(jnp.concatenate([at_s[c], rt_s[c]], axis=0),
                   jnp.concatenate([stack(bt[rows[c]]), stack(kt[rows[c]])], axis=0)) for c in cs]
    serial_link()
    n_ab = [jnp.where(strict, gram[c][:pair, :pair], 0.0) for c in cs]
    a_ak = [jnp.where(strict, gram[c][:pair, pair:], 0.0).astype(BF16) for c in cs]
    a_r = [jnp.concatenate([jnp.where(incl, gram[c][pair:, :pair], 0.0),
                            jnp.where(incl, gram[c][pair:, pair:], 0.0)], axis=1).astype(BF16) for c in cs]
    x1 = [_mm(a_ak[c], v_s[c]) for c in cs]
    serial_link()
    tinv = [eye + n_ab[c] for c in cs]
    pw = [n_ab[c].astype(BF16) for c in cs]
    pw = [_mm(pw[c], pw[c]).astype(BF16) for c in cs]
    serial_link()
    span = 4
    while span < cl:
        both = [_mm(jnp.concatenate([pw[c], tinv[c].astype(BF16)], axis=0), pw[c]) for c in cs]
        pw = [both[c][:pair].astype(BF16) for c in cs]
        tinv = [tinv[c] + both[c][pair:] for c in cs]
        span *= 2
        serial_link()
    tinv = [tinv[c] + _mm(tinv[c], pw[c]) for c in cs]
    finish_mean()
    tz = [_mm(tinv[c], jnp.concatenate([at_s[c], x1[c].astype(BF16)], axis=1)) for c in cs]
    finish_var()
    zeros = jnp.zeros((pair, LANES), BF16)
    q = [_mm(a_r[c], jnp.concatenate([tz[c].astype(BF16), jnp.concatenate([zeros, v_s[c]], axis=1)], axis=0))
         for c in cs]
    finish_store()
    rp = [rt_s[c].astype(F32) + q[c][:, :LANES] for c in cs]
    mc = [_mm_tn(tz[c], bl_s[c]) for c in cs]
    vk = [_mm_tn(v_s[c], kl_s[c]) for c in cs]

    for c in cs:
        rp_st[slot_w, c] = rp[c].astype(BF16)
        yp_st[slot_w, c] = q[c][:, LANES:]
        mt_st[slot_w, c] = mc[c][:LANES].astype(BF16)
        ct_st[slot_w, c] = mc[c][LANES:] + vk[c]
        pl_st[slot_w, c] = jnp.broadcast_to(last[c], (8, LANES))


def _rwscan(r, lw, k, v, a, b, g, vecs, batch, seq, lt=512, pairs=2):
    t, d = r.shape
    nt = seq // lt
    width = pairs * LANES
    items = pairs * (lt // RW_CHUNK)
    pair = 2 * RW_CHUNK
    blk_in = pl.BlockSpec((lt, width), lambda bi, hp, ti: (bi * nt + jnp.minimum(ti, nt - 1), hp))
    blk_out = pl.BlockSpec((lt, width), lambda bi, hp, ti: (bi * nt + jnp.maximum(ti - 1, 0), hp))
    return pl.pallas_call(
        functools.partial(_rwscan_kernel, chunks=lt // RW_CHUNK),
        grid=(batch, d // width, nt + 1),
        in_specs=[blk_in] * 7 + [pl.BlockSpec((8, width), lambda bi, hp, ti: (0, hp))],
        out_specs=blk_out,
        out_shape=jax.ShapeDtypeStruct((t, d), BF16),
        scratch_shapes=[
            pltpu.VMEM((pairs, LANES, LANES), F32),
            pltpu.VMEM((2, items, pair, LANES), BF16),
            pltpu.VMEM((2, items, pair, LANES), F32),
            pltpu.VMEM((2, items, LANES, LANES), BF16),
            pltpu.VMEM((2, items, LANES, LANES), F32),
            pltpu.VMEM((2, items, 8, LANES), F32),
            pltpu.VMEM((2, lt, width), F32),
            pltpu.VMEM((2, lt, width), F32),
        ],
        compiler_params=_params("parallel", "parallel", "arbitrary"),
        name="rwscan",
    )(r, lw, k, v, a, b, g, vecs)


def _rope_kernel(pos_ref, inv_ref, cos_ref, sin_ref):
    ang = pos_ref[...].astype(F32) * inv_ref[...]
    cos_ref[...] = jnp.cos(ang)
    sin_ref[...] = jnp.sin(ang)


def _rope_tables(pos, inv_freq, tm=512):
    t = pos.shape[0]
    half = inv_freq.shape[1]
    out = jax.ShapeDtypeStruct((t, half), F32)
    return pl.pallas_call(
        _rope_kernel,
        grid=(t // tm,),
        in_specs=[pl.BlockSpec((tm, 1), lambda i: (i, 0)), pl.BlockSpec((1, half), lambda i: (0, 0))],
        out_specs=[pl.BlockSpec((tm, half), lambda i: (i, 0))] * 2,
        out_shape=[out, out],
        compiler_params=_params("parallel"),
        name="rope",
    )(pos, inv_freq)


def _retscan_kernel(q_ref, k_ref, v_ref, gate_ref, cos_ref, sin_ref, lg_ref, gn_ref, o_ref, st_ref, intra_ref):
    lc, dk = q_ref.shape
    lg = lg_ref[0][:, 0:1]
    ic = lax.broadcasted_iota(jnp.int32, (lc, 1), 0).astype(F32)

    @pl.when(pl.program_id(2) == 0)
    def _():
        st_ref[...] = jnp.zeros_like(st_ref)
        diff = ic - lax.broadcasted_iota(jnp.int32, (1, lc), 1).astype(F32)
        causal = diff >= 0
        intra_ref[...] = jnp.where(causal, jnp.exp(lg * jnp.where(causal, diff, 0.0)), 0.0)

    cos, sin = cos_ref[...], sin_ref[...]
    half = dk // 2

    def rot(t):
        t1, t2 = t[:, :half], t[:, half:]
        return jnp.concatenate([t1 * cos - t2 * sin, t1 * sin + t2 * cos], axis=1)

    q = rot(q_ref[...].astype(F32))
    k = rot(k_ref[...].astype(F32)) * (dk ** -0.5)
    v = v_ref[...]
    xi = jnp.exp(lg * (ic + 1.0))
    zeta = jnp.exp(lg * (lc - 1.0 - ic))
    st = st_ref[...]
    sc = _mm_nt(q, k) * intra_ref[...]
    y = _mm(sc, v) + _mm(q, st) * xi
    st_ref[...] = jnp.exp(lg * lc) * st + _mm_tn(k * zeta, v)
    mean = jnp.mean(y, axis=-1, keepdims=True)
    yc = y - mean
    yn = yc * lax.rsqrt(jnp.mean(yc * yc, axis=-1, keepdims=True) + NORM_EPS) * gn_ref[...]
    gate = gate_ref[...].astype(F32)
    o_ref[...] = (gate * _sigmoid(gate) * yn).astype(BF16)


def _retscan(proj, cos, sin, lg, gn_w, batch, seq, d, lc=256):
    t = proj.shape[0]
    nc = seq // lc
    dk, dv = d // RET_HEADS, 2 * d // RET_HEADS
    tok = lambda bi, h, ci: bi * nc + ci
    return pl.pallas_call(
        _retscan_kernel,
        grid=(batch, RET_HEADS, nc),
        in_specs=[
            pl.BlockSpec((lc, dk), lambda bi, h, ci: (tok(bi, h, ci), h)),
            pl.BlockSpec((lc, dk), lambda bi, h, ci: (tok(bi, h, ci), RET_HEADS + h)),
            pl.BlockSpec((lc, dv), lambda bi, h, ci: (tok(bi, h, ci), RET_HEADS + h)),
            pl.BlockSpec((lc, dv), lambda bi, h, ci: (tok(bi, h, ci), 2 * RET_HEADS + h)),
            pl.BlockSpec((lc, dk // 2), lambda bi, h, ci: (tok(bi, h, ci), 0)),
            pl.BlockSpec((lc, dk // 2), lambda bi, h, ci: (tok(bi, h, ci), 0)),
            pl.BlockSpec((1, 1, LANES), lambda bi, h, ci: (h, 0, 0)),
            pl.BlockSpec((1, dv), lambda bi, h, ci: (0, h)),
        ],
        out_specs=pl.BlockSpec((lc, dv), lambda bi, h, ci: (tok(bi, h, ci), h)),
        out_shape=jax.ShapeDtypeStruct((t, 2 * d), BF16),
        scratch_shapes=[pltpu.VMEM((dk, dv), F32), pltpu.VMEM((lc, lc), F32)],
        compiler_params=_params("parallel", "parallel", "arbitrary"),
        name="retscan",
    )(proj, proj, proj, proj, cos, sin, lg, gn_w)


def _mlscan_kernel(q_ref, k_ref, v_ref, og_ref, gt_ref, bif_ref, hn_ref, o_ref, c_ref, m_ref):
    lc = q_ref.shape[0]
    dqk = q_ref.shape[1] // ML_HEADS
    dv = v_ref.shape[1] // ML_HEADS

    @pl.when(pl.program_id(1) == 0)
    def _():
        c_ref[...] = jnp.zeros_like(c_ref)
        m_ref[...] = jnp.zeros_like(m_ref)

    gt = gt_ref[...] + bif_ref[...]
    gt = ML_GATE_CAP * jnp.tanh(gt * (1.0 / ML_GATE_CAP))
    lane = lax.broadcasted_iota(jnp.int32, gt.shape, 1)
    is_f = (lane >= ML_HEADS) & (lane < 2 * ML_HEADS)
    log_f = jnp.where(is_f, -_softplus(-gt), 0.0)
    bcum = pltpu.roll(_cumsum_rows(log_f), LANES - ML_HEADS, 1)
    rj_t = (bcum - gt).T
    ir = lax.broadcasted_iota(jnp.int32, (lc, lc), 0)
    ic = lax.broadcasted_iota(jnp.int32, (lc, lc), 1)
    causal = ir >= ic
    ones_col = jnp.where(lax.broadcasted_iota(jnp.int32, (lc, LANES), 1) == 0, 1.0, 0.0).astype(BF16)
    for h in range(ML_HEADS):
        q = q_ref[:, h * dqk:(h + 1) * dqk]
        k = k_ref[:, h * dqk:(h + 1) * dqk].astype(F32) * (dqk ** -0.5)
        vext = jnp.concatenate([v_ref[:, h * dv:(h + 1) * dv], ones_col], axis=1)
        b_col = bcum[:, h:h + 1]
        i_col = gt[:, h:h + 1]
        m_st = m_ref[h:h + 1, 0:1]
        log_d = jnp.where(causal, b_col - rj_t[h:h + 1, :], -jnp.inf)
        log_inter = b_col + m_st
        m_t = jnp.maximum(log_inter, jnp.max(log_d, axis=-1, keepdims=True))
        dmat = jnp.exp(log_d - m_t)
        w_inter = jnp.exp(log_inter - m_t)
        sc = _mm_nt(q, k) * dmat
        c_st = c_ref[h]
        num = _mm(sc, vext) + w_inter * _mm(q, c_st)
        dot = num[:, dv:dv + 1]
        hc = num[:, :dv] / jnp.maximum(jnp.abs(dot), jnp.exp(-m_t))
        m_new = m_t[lc - 1:lc, :]
        b_last = b_col[lc - 1:lc, :]
        w_s = jnp.exp(b_last - b_col + i_col - m_new)
        dec = jnp.exp(b_last + m_st - m_new)
        c_ref[h] = dec * c_st + _mm_tn(k * w_s, vext)
        m_ref[h:h + 1, :] = jnp.broadcast_to(m_new, (1, LANES))
        yf = hc * lax.rsqrt(jnp.mean(hc * hc, axis=-1, keepdims=True) + NORM_EPS)
        og = og_ref[:, h * dv:(h + 1) * dv].astype(F32)
        o_ref[:, h * dv:(h + 1) * dv] = (_sigmoid(og) * (yf * hn_ref[:, h * dv:(h + 1) * dv])).astype(BF16)


def _mlscan(proj, gates, bif, hn_w, batch, seq, d, lc=256):
    t = proj.shape[0]
    nc = seq // lc
    dq = d // 2
    tok = lambda bi, ci: bi * nc + ci
    return pl.pallas_call(
        _mlscan_kernel,
        grid=(batch, nc),
        in_specs=[
            pl.BlockSpec((lc, dq), lambda bi, ci: (tok(bi, ci), 0)),
            pl.BlockSpec((lc, dq), lambda bi, ci: (tok(bi, ci), 1)),
            pl.BlockSpec((lc, d), lambda bi, ci: (tok(bi, ci), 1)),
            pl.BlockSpec((lc, d), lambda bi, ci: (tok(bi, ci), 2)),
            pl.BlockSpec((lc, LANES), lambda bi, ci: (tok(bi, ci), 0)),
            pl.BlockSpec((1, LANES), lambda bi, ci: (0, 0)),
            pl.BlockSpec((1, d), lambda bi, ci: (0, 0)),
        ],
        out_specs=pl.BlockSpec((lc, d), lambda bi, ci: (tok(bi, ci), 0)),
        out_shape=jax.ShapeDtypeStruct((t, d), BF16),
        scratch_shapes=[pltpu.VMEM((ML_HEADS, dq // ML_HEADS, d // ML_HEADS + LANES), F32),
                        pltpu.VMEM((8, LANES), F32)],
        compiler_params=_params("parallel", "arbitrary"),
        name="mlscan",
    )(proj, proj, proj, proj, gates, bif, hn_w)


def _pad_cols(w, n):
    return jnp.pad(w, ((0, 0), (0, n - w.shape[1])))


def _pad_rows(w, n):
    return jnp.pad(w, ((0, n - w.shape[0]), (0, 0)))


def kernel(x, positions, norm_mix, norm_ffn, norm_final, rw_mu, rw_w_rkv, rw_w0, rw_w1, rw_w2, rw_a0, rw_a1, rw_a2, rw_g1, rw_g2, rw_k_k, rw_k_a, rw_r_k, rw_ln_w, rw_ln_b, rw_w_o, rw_v0, rw_v1, rw_v2, ret_w_in, ret_gn_w, ret_w_out, ml_w_in, ml_b_if, ml_hn_w, ml_w_out, ffn_w1, ffn_w3, ffn_w2):
    batch, seq, d = x.shape
    t = batch * seq
    depth = norm_mix.shape[0]
    xf = x.reshape(t, d)
    row = lambda vec: vec.reshape(1, -1).astype(F32)

    dk = d // RET_HEADS
    inv_freq = (1.0 / (ROPE_BASE ** jnp.linspace(0.0, 1.0, dk // 2, dtype=F32))).reshape(1, -1)
    cos, sin = _rope_tables(positions.reshape(t, 1), inv_freq)
    log_gamma = jnp.log1p(-jnp.exp2(-5.0 - jnp.arange(RET_HEADS, dtype=F32)))
    lg = jnp.broadcast_to(log_gamma[:, None, None], (RET_HEADS, 1, LANES))

    v_first = None
    for i in range(depth):
        kind, j = i % 3, i // 3
        g_mix = row(norm_mix[i])
        if kind == 0:
            has_vres = j > 0
            zeros_v1 = jnp.zeros((d, RW_LORA_G - RW_LORA_V), F32)
            zeros_v2 = jnp.zeros((RW_LORA_G - RW_LORA_V, d), F32)
            lin = jnp.concatenate([
                _pad_cols(rw_w1[j], RW_LORA_A - RW_LORA_W), _pad_cols(rw_a1[j], RW_LORA_V - RW_LORA_A),
                _pad_cols(rw_v1[j - 1], RW_LORA_G - RW_LORA_V) if has_vres else zeros_v1,
                rw_g1[j]], axis=1).astype(BF16)
            lout = jnp.concatenate([
                _pad_rows(rw_w2[j], RW_LORA_A - RW_LORA_W), _pad_rows(rw_a2[j], RW_LORA_V - RW_LORA_A),
                _pad_rows(rw_v2[j - 1], RW_LORA_G - RW_LORA_V) if has_vres else zeros_v2,
                rw_g2[j]], axis=0).astype(BF16)
            v0 = rw_v0[j - 1] if has_vres else jnp.zeros((d,), F32)
            zero = jnp.zeros((d,), F32)
            vecs = jnp.stack([rw_w0[j], rw_a0[j], v0, rw_k_k[j], rw_k_a[j], zero, zero, zero]).astype(F32)
            r, lw, k, v, a, b, g = _rwproj(xf, g_mix, rw_mu[j].astype(F32), rw_w_rkv[j].astype(BF16), lin, lout,
                                           vecs, v_first if has_vres else None, seq)
            if j == 0:
                v_first = v
            svecs = jnp.stack([rw_r_k[j].reshape(-1), rw_ln_w[j], rw_ln_b[j], zero, zero, zero, zero, zero]).astype(F32)
            z = _rwscan(r, lw, k, v, a, b, g, svecs, batch, seq)
            xf = _outproj(z, rw_w_o[j].astype(BF16), xf)
        elif kind == 1:
            (proj,) = _normproj(xf, g_mix, ret_w_in[j].astype(BF16), tn=1024)
            z = _retscan(proj, cos, sin, lg, row(ret_gn_w[j]), batch, seq, d)
            xf = _outproj(z, ret_w_out[j].astype(BF16), xf)
        else:
            n_in = ml_w_in.shape[2]
            n_pad = 3 * d + LANES
            w_in = _pad_cols(ml_w_in[j], n_pad).astype(BF16)
            proj, gates = _normproj(xf, g_mix, w_in, tn=n_pad // 7, side_cols=LANES)
            bif = _pad_cols(ml_b_if[j].reshape(1, -1).astype(F32), LANES)
            z = _mlscan(proj, gates, bif, row(ml_hn_w[j]), batch, seq, d)
            xf = _outproj(z, ml_w_out[j].astype(BF16), xf)
        xf = _ffn(xf, row(norm_ffn[i]), ffn_w1[i].astype(BF16), ffn_w3[i].astype(BF16), ffn_w2[i].astype(BF16),
                  row(norm_final), final_norm=(i == depth - 1))
    return xf.reshape(batch, seq, d)
```

```python
import functools

import jax
import jax.numpy as jnp
from jax import lax
from jax.experimental import pallas as pl
from jax.experimental.pallas import tpu as pltpu

F32 = jnp.float32
BF16 = jnp.bfloat16

NORM_EPS = 1e-6
RW_HEAD = 64
RW_GN_EPS = 64e-5
RW_CHUNK = 64
RET_HEADS = 8
ROPE_BASE = 10000.0
ML_HEADS = 4
ML_GATE_CAP = 15.0
LANES = 128
NORM_ROWS = 256
VMEM_LIMIT = 56 * 1024 * 1024


def _params(*sem):
    return pltpu.CompilerParams(dimension_semantics=sem, vmem_limit_bytes=VMEM_LIMIT)


def _mm(a, b):
    return jnp.dot(a.astype(BF16), b.astype(BF16), preferred_element_type=F32)


def _mm_nt(a, b):
    return lax.dot_general(a.astype(BF16), b.astype(BF16), (((1,), (1,)), ((), ())),
                           preferred_element_type=F32)


def _mm_tn(a, b):
    return lax.dot_general(a.astype(BF16), b.astype(BF16), (((0,), (0,)), ((), ())),
                           preferred_element_type=F32)


def _rms(x, g):
    return x * lax.rsqrt(jnp.mean(x * x, axis=-1, keepdims=True) + NORM_EPS) * g


def _sigmoid(x):
    return 0.5 * jnp.tanh(0.5 * x) + 0.5


def _softplus(x):
    return jnp.maximum(x, 0.0) + jnp.log(1.0 + jnp.exp(-jnp.abs(x)))


def _cumsum_rows(x):
    n = x.shape[0]
    row = lax.broadcasted_iota(jnp.int32, x.shape, 0)
    s = 1
    while s < n:
        x = x + jnp.where(row >= s, pltpu.roll(x, s, 0), 0.0)
        s *= 2
    return x


def _seg64_sum(x, split=True):
    r = lax.broadcasted_iota(jnp.int32, (LANES, LANES), 0) // RW_HEAD
    c = lax.broadcasted_iota(jnp.int32, (LANES, LANES), 1) // RW_HEAD
    bd = jnp.where(r == c, 1.0, 0.0).astype(BF16)
    hi = x.astype(BF16)
    out = jnp.dot(hi, bd, preferred_element_type=F32)
    if split:
        out = out + jnp.dot((x - hi.astype(F32)).astype(BF16), bd, preferred_element_type=F32)
    return out


def _ffn_kernel(x_ref, g_ref, w1_ref, w3_ref, w2_ref, gf_ref, o_ref, h_s, *, final_norm):
    j = pl.program_id(1)
    tm = x_ref.shape[0]
    half = w1_ref.shape[1] // 2

    @pl.when(j == 0)
    def _():
        for s in range(0, tm, NORM_ROWS):
            x = x_ref[s:s + NORM_ROWS, :]
            h_s[s:s + NORM_ROWS, :] = _rms(x, g_ref[...]).astype(BF16)
            o_ref[s:s + NORM_ROWS, :] = x

    h = h_s[...]
    gate_up = [(jnp.dot(h, w1_ref[:, c:c + half], preferred_element_type=F32),
                jnp.dot(h, w3_ref[:, c:c + half], preferred_element_type=F32)) for c in (0, half)]
    down = None
    for n, (a, b) in enumerate(gate_up):
        act = (a * _sigmoid(a) * b).astype(BF16)
        part = jnp.dot(act, w2_ref[n * half:(n + 1) * half, :], preferred_element_type=F32)
        down = part if down is None else down + part
    o_ref[...] += down

    if final_norm:
        @pl.when(j == pl.num_programs(1) - 1)
        def _():
            for s in range(0, tm, NORM_ROWS):
                o_ref[s:s + NORM_ROWS, :] = _rms(o_ref[s:s + NORM_ROWS, :], gf_ref[...])


def _ffn(x, g, w1, w3, w2, gf, final_norm, tm=1024, tf=512):
    t, d = x.shape
    ff = w1.shape[1]
    tf = min(tf, ff)
    return pl.pallas_call(
        functools.partial(_ffn_kernel, final_norm=final_norm),
        grid=(t // tm, ff // tf),
        in_specs=[
            pl.BlockSpec((tm, d), lambda i, j: (i, 0), pipeline_mode=pl.Buffered(1)),
            pl.BlockSpec((1, d), lambda i, j: (0, 0)),
            pl.BlockSpec((d, tf), lambda i, j: (0, j)),
            pl.BlockSpec((d, tf), lambda i, j: (0, j)),
            pl.BlockSpec((tf, d), lambda i, j: (j, 0)),
            pl.BlockSpec((1, d), lambda i, j: (0, 0)),
        ],
        out_specs=pl.BlockSpec((tm, d), lambda i, j: (i, 0)),
        out_shape=jax.ShapeDtypeStruct((t, d), F32),
        scratch_shapes=[pltpu.VMEM((tm, d), BF16)],
        compiler_params=_params("parallel", "arbitrary"),
        name="ffn",
    )(x, g, w1, w3, w2, gf)


def _outproj_kernel(z_ref, w_ref, x_ref, o_ref):
    o_ref[...] = x_ref[...] + jnp.dot(z_ref[...], w_ref[...], preferred_element_type=F32)


OUTPROJ_W_BYTES = 8 * 1024 * 1024


def _outproj(z, w, x, tm=512):
    t, k = z.shape
    d = w.shape[1]
    tn = min(d, OUTPROJ_W_BYTES // (2 * k))
    return pl.pallas_call(
        _outproj_kernel,
        grid=(d // tn, t // tm),
        in_specs=[
            pl.BlockSpec((tm, k), lambda j, i: (i, 0)),
            pl.BlockSpec((k, tn), lambda j, i: (0, j)),
            pl.BlockSpec((tm, tn), lambda j, i: (i, j)),
        ],
        out_specs=pl.BlockSpec((tm, tn), lambda j, i: (i, j)),
        out_shape=jax.ShapeDtypeStruct((t, d), F32),
        compiler_params=_params("arbitrary", "arbitrary"),
        name="outproj",
    )(z, w, x)


def _normproj_kernel(x_ref, g_ref, w_ref, o_ref, *rest, side_cols):
    h_s = rest[-1]

    @pl.when(pl.program_id(1) == 0)
    def _():
        for s in range(0, x_ref.shape[0], NORM_ROWS):
            h_s[s:s + NORM_ROWS, :] = _rms(x_ref[s:s + NORM_ROWS, :], g_ref[...]).astype(BF16)

    acc = jnp.dot(h_s[...], w_ref[...], preferred_element_type=F32)
    o_ref[...] = acc.astype(o_ref.dtype)
    if side_cols:
        @pl.when(pl.program_id(1) == pl.num_programs(1) - 1)
        def _():
            rest[0][...] = acc[:, side_cols[0]:side_cols[1]]


def _normproj(x, g, w, tn, side_cols=None, tm=1024):
    t, d = x.shape
    n = w.shape[1]
    out_specs = [pl.BlockSpec((tm, tn), lambda i, j: (i, j))]
    out_shape = [jax.ShapeDtypeStruct((t, n), BF16)]
    if side_cols:
        width = side_cols[1] - side_cols[0]
        out_specs.append(pl.BlockSpec((tm, width), lambda i, j: (i, 0)))
        out_shape.append(jax.ShapeDtypeStruct((t, width), F32))
    return pl.pallas_call(
        functools.partial(_normproj_kernel, side_cols=side_cols),
        grid=(t // tm, n // tn),
        in_specs=[
            pl.BlockSpec((tm, d), lambda i, j: (i, 0)),
            pl.BlockSpec((1, d), lambda i, j: (0, 0)),
            pl.BlockSpec((d, tn), lambda i, j: (0, j)),
        ],
        out_specs=out_specs,
        out_shape=out_shape,
        scratch_shapes=[pltpu.VMEM((tm, d), BF16)],
        compiler_params=_params("parallel", "arbitrary"),
        name="normproj",
    )(x, g, w)


RW_LORA_W = 0
RW_LORA_A = 128
RW_LORA_V = 256
RW_LORA_G = 384
RW_LORA_END = 640
RW_PROLOGUE_ROWS = 128


def _rwproj_kernel(*refs, tiles_per_seq, has_vres):
    if has_vres:
        (x_ref, xp_ref, g_ref, mu_ref, wrkv_ref, lin_ref, lout_ref, vec_ref, vf_ref,
         r_o, lw_o, k_o, v_o, a_o, b_o, g_o, xr_s, xk_s, xv_s, l1_s) = refs
    else:
        (x_ref, xp_ref, g_ref, mu_ref, wrkv_ref, lin_ref, lout_ref, vec_ref,
         r_o, lw_o, k_o, v_o, a_o, b_o, g_o, xr_s, xk_s, xv_s, l1_s) = refs
        vf_ref = None
    i = pl.program_id(0)

    @pl.when(pl.program_id(1) == 0)
    def _():
        g = g_ref[...]
        mu = mu_ref[...]
        sub = RW_PROLOGUE_ROWS
        for s in range(x_ref.shape[0] // sub):
            rows = slice(s * sub, (s + 1) * sub)
            h = _rms(x_ref[rows, :], g)
            if s == 0:
                hp = _rms(xp_ref[7:8, :], g)
                hp = jnp.where(i % tiles_per_seq == 0, 0.0, hp)
            else:
                hp = _rms(x_ref[s * sub - 8:s * sub, :], g)[7:8, :]
            row = lax.broadcasted_iota(jnp.int32, h.shape, 0)
            xx = jnp.where(row == 0, hp, pltpu.roll(h, 1, 0)) - h
            xr_s[rows, :] = (h + xx * mu[0:1]).astype(BF16)
            xk_s[rows, :] = (h + xx * mu[2:3]).astype(BF16)
            xv = (h + xx * mu[3:4]).astype(BF16)
            xv_s[rows, :] = xv
            xw = h + xx * mu[1:2]
            xa = h + xx * mu[4:5]
            xg = h + xx * mu[5:6]
            l1_s[rows, RW_LORA_W:RW_LORA_A] = jnp.tanh(_mm(xw, lin_ref[:, RW_LORA_W:RW_LORA_A])).astype(BF16)
            l1_s[rows, RW_LORA_A:RW_LORA_V] = _mm(xa, lin_ref[:, RW_LORA_A:RW_LORA_V]).astype(BF16)
            l1_s[rows, RW_LORA_V:RW_LORA_G] = _mm(xv, lin_ref[:, RW_LORA_V:RW_LORA_G]).astype(BF16)
            l1_s[rows, RW_LORA_G:RW_LORA_END] = _sigmoid(
                _mm(xg, lin_ref[:, RW_LORA_G:RW_LORA_END])).astype(BF16)

    r = jnp.dot(xr_s[...], wrkv_ref[0], preferred_element_type=F32)
    k = jnp.dot(xk_s[...], wrkv_ref[1], preferred_element_type=F32)
    v = jnp.dot(xv_s[...], wrkv_ref[2], preferred_element_type=F32)
    vec = vec_ref[...]
    w0, a0, v0, k_k, k_a = (vec[n:n + 1] for n in range(5))

    def lora2(lo, hi):
        return jnp.dot(l1_s[:, lo:hi], lout_ref[lo:hi, :], preferred_element_type=F32)

    w_log = -_softplus(-(w0 + lora2(RW_LORA_W, RW_LORA_A))) - 0.5
    lw_o[...] = -jnp.exp(w_log)
    a = _sigmoid(a0 + lora2(RW_LORA_A, RW_LORA_V))
    if has_vres:
        v = v + (vf_ref[...].astype(F32) - v) * _sigmoid(v0 + lora2(RW_LORA_V, RW_LORA_G))
    g_o[...] = lora2(RW_LORA_G, RW_LORA_END).astype(g_o.dtype)
    kk = k * k_k
    tn = kk.shape[1]
    ss = jnp.concatenate([_seg64_sum(jnp.square(kk[:, c:c + LANES]), split=False) for c in range(0, tn, LANES)],
                         axis=1)
    kk = kk * lax.rsqrt(jnp.maximum(ss, 1e-24))
    r_o[...] = r.astype(r_o.dtype)
    k_o[...] = (k * (1.0 + (a - 1.0) * k_a)).astype(k_o.dtype)
    v_o[...] = v.astype(v_o.dtype)
    a_o[...] = (-kk).astype(a_o.dtype)
    b_o[...] = (kk * a).astype(b_o.dtype)


def _rwproj(x, g, mu, wrkv, lin, lout, vecs, vfirst, seq, tm=512, tn=512):
    t, d = x.shape
    has_vres = vfirst is not None
    rows8 = tm // 8
    in_specs = [
        pl.BlockSpec((tm, d), lambda i, j: (i, 0)),
        pl.BlockSpec((8, d), lambda i, j: (jnp.maximum(i * rows8 - 1, 0), 0)),
        pl.BlockSpec((1, d), lambda i, j: (0, 0)),
        pl.BlockSpec((6, d), lambda i, j: (0, 0)),
        pl.BlockSpec((3, d, tn), lambda i, j: (0, 0, j)),
        pl.BlockSpec((d, RW_LORA_END), lambda i, j: (0, 0)),
        pl.BlockSpec((RW_LORA_END, tn), lambda i, j: (0, j)),
        pl.BlockSpec((8, tn), lambda i, j: (0, j)),
    ]
    args = [x, x, g, mu, wrkv, lin, lout, vecs]
    if has_vres:
        in_specs.append(pl.BlockSpec((tm, tn), lambda i, j: (i, j)))
        args.append(vfirst)
    out_dtypes = [BF16, F32, BF16, BF16, BF16, BF16, BF16]
    return pl.pallas_call(
        functools.partial(_rwproj_kernel, tiles_per_seq=seq // tm, has_vres=has_vres),
        grid=(t // tm, d // tn),
        in_specs=in_specs,
        out_specs=[pl.BlockSpec((tm, tn), lambda i, j: (i, j))] * 7,
        out_shape=[jax.ShapeDtypeStruct((t, d), dt) for dt in out_dtypes],
        scratch_shapes=[pltpu.VMEM((tm, d), BF16)] * 3 + [pltpu.VMEM((tm, RW_LORA_END), BF16)],
        compiler_params=_params("parallel", "arbitrary"),
        name="rwproj",
    )(*args)


def _rwscan_kernel(r_ref, lw_ref, k_ref, v_ref, a_ref, b_ref, g_ref, vec_ref, o_ref,
                   s_ref, rp_st, yp_st, mt_st, ct_st, pl_st, bg_st, g_st, *, chunks):
    cl = RW_CHUNK
    pair = 2 * cl
    lt = chunks * cl
    width = lw_ref.shape[1]
    pairs = width // LANES
    ti = pl.program_id(2)
    slot_w = ti % 2
    slot_r = 1 - slot_w

    @pl.when(ti == 0)
    def _():
        s_ref[...] = jnp.zeros_like(s_ref)
        for ref in (rp_st, yp_st, mt_st, ct_st, pl_st, bg_st, g_st):
            ref[1] = jnp.zeros(ref.shape[1:], ref.dtype)

    vec = vec_ref[...]
    states = [s_ref[p] for p in range(pairs)]
    ys = [[] for _ in range(pairs)]
    links_done = [0]

    def serial_link():
        j = links_done[0]
        if j >= chunks:
            return
        links_done[0] = j + 1
        for p in range(pairs):
            c = p * chunks + j
            st = states[p]
            y_s = _mm_nt(rp_st[slot_r, c], st) + yp_st[slot_r, c]
            ys[p].append(y_s[:cl] + y_s[cl:])
            states[p] = jnp.exp(pl_st[slot_r, c][0:1]) * st + _mm(st, mt_st[slot_r, c]) + ct_st[slot_r, c]

    norm = {}

    def finish_mean():
        while links_done[0] < chunks:
            serial_link()
        for p in range(pairs):
            s_ref[p] = states[p]
        norm["y"] = [jnp.concatenate(ys[p], axis=0) for p in range(pairs)]
        norm["mean"] = [_seg64_sum(y) * (1.0 / RW_HEAD) for y in norm["y"]]

    def finish_var():
        norm["yc"] = [norm["y"][p] - norm["mean"][p] for p in range(pairs)]
        norm["var"] = [_seg64_sum(yc * yc, split=False) * (1.0 / RW_HEAD) for yc in norm["yc"]]

    def finish_store():
        for p in range(pairs):
            lanes = slice(p * LANES, (p + 1) * LANES)
            yn = norm["yc"][p] * lax.rsqrt(norm["var"][p] + RW_GN_EPS) * vec[1:2, lanes] + vec[2:3, lanes]
            o_ref[:, lanes] = (yn * g_st[slot_r, :, lanes] + bg_st[slot_r, :, lanes]).astype(BF16)

    serial_link()

    head0 = lax.broadcasted_iota(jnp.int32, (cl, LANES), 1) < RW_HEAD

    def stack(x):
        return jnp.concatenate([jnp.where(head0, x, 0.0), jnp.where(head0, 0.0, x)], axis=0).astype(BF16)

    ri = lax.broadcasted_iota(jnp.int32, (pair, pair), 0)
    ci = lax.broadcasted_iota(jnp.int32, (pair, pair), 1)
    same = (ri // cl) == (ci // cl)
    strict = same & ((ri % cl) > (ci % cl))
    incl = same & ((ri % cl) >= (ci % cl))
    eye = jnp.where(ri == ci, 1.0, 0.0)

    lw = lw_ref[...]
    rowc = lax.broadcasted_iota(jnp.int32, (lt, width), 0) % cl
    cum = lw
    s = 1
    while s < cl:
        cum = cum + jnp.where(rowc >= s, pltpu.roll(cum, s, 0), 0.0)
        s *= 2
    r, k, v = r_ref[...].astype(F32), k_ref[...].astype(F32), v_ref[...].astype(F32)
    a, b = a_ref[...].astype(F32), b_ref[...].astype(F32)
    g = g_ref[...].astype(F32)
    rkr = r * k * vec[0:1]
    bonus = jnp.concatenate([_seg64_sum(rkr[:, p * LANES:(p + 1) * LANES], split=False) for p in range(pairs)],
                            axis=1)
    bg_st[slot_w] = bonus * v * g
    g_st[slot_w] = g
    e_neg = jnp.exp(-cum)
    rt = r * jnp.exp(cum)
    at = a * jnp.exp(cum - lw)
    bt = b * e_neg
    kt = k * e_neg

    cs = range(pairs * chunks)
    rows = [(slice((c % chunks) * cl, (c % chunks + 1) * cl),
             slice((c // chunks) * LANES, (c // chunks + 1) * LANES)) for c in cs]
    last = [cum[rows[c][0], rows[c][1]][cl - 1:cl, :] for c in cs]
    tail = [jnp.exp(last[c] - cum[rows[c]]) for c in cs]
    at_s = [stack(at[rows[c]]) for c in cs]
    rt_s = [stack(rt[rows[c]]) for c in cs]
    v_s = [stack(v[rows[c]]) for c in cs]
    bl_s = [stack(b[rows[c]] * tail[c]) for c in cs]
    kl_s = [stack(k[rows[c]] * tail[c]) for c in cs]
    gram = [_mm_nt(jnp.concatenate([at_s[c], rt_s[c]], axis=0),
                   jnp.concatenate([stack(bt[rows[c]]), stack(kt[rows[c]])], axis=0)) for c in cs]
    serial_link()
    n_ab = [jnp.where(strict, gram[c][:pair, :pair], 0.0) for c in cs]
    a_ak = [jnp.where(strict, gram[c][:pair, pair:], 0.0).astype(BF16) for c in cs]
    a_r = [jnp.concatenate([jnp.where(incl, gram[c][pair:, :pair], 0.0),
                            jnp.where(incl, gram[c][pair:, pair:], 0.0)], axis=1).astype(BF16) for c in cs]
    x1 = [_mm(a_ak[c], v_s[c]) for c in cs]
    serial_link()
    tinv = [eye + n_ab[c] for c in cs]
    pw = [n_ab[c].astype(BF16) for c in cs]
    pw = [_mm(pw[c], pw[c]).astype(BF16) for c in cs]
    serial_link()
    span = 4
    while span < cl:
        both = [_mm(jnp.concatenate([pw[c], tinv[c].astype(BF16)], axis=0), pw[c]) for c in cs]
        pw = [both[c][:pair].astype(BF16) for c in cs]
        tinv = [tinv[c] + both[c][pair:] for c in cs]
        span *= 2
        serial_link()
    tinv = [tinv[c] + _mm(tinv[c], pw[c]) for c in cs]
    finish_mean()
    tz = [_mm(tinv[c], jnp.concatenate([at_s[c], x1[c].astype(BF16)], axis=1)) for c in cs]
    finish_var()
    zeros = jnp.zeros((pair, LANES), BF16)
    q = [_mm(a_r[c], jnp.concatenate([tz[c].astype(BF16), jnp.concatenate([zeros, v_s[c]], axis=1)], axis=0))
         for c in cs]
    finish_store()
    rp = [rt_s[c].astype(F32) + q[c][:, :LANES] for c in cs]
    mc = [_mm_tn(tz[c], bl_s[c]) for c in cs]
    vk = [_mm_tn(v_s[c], kl_s[c]) for c in cs]

    for c in cs:
        rp_st[slot_w, c] = rp[c].astype(BF16)
        yp_st[slot_w, c] = q[c][:, LANES:]
        mt_st[slot_w, c] = mc[c][:LANES].astype(BF16)
        ct_st[slot_w, c] = mc[c][LANES:] + vk[c]
        pl_st[slot_w, c] = jnp.broadcast_to(last[c], (8, LANES))


def _rwscan(r, lw, k, v, a, b, g, vecs, batch, seq, lt=512, pairs=2):
    t, d = r.shape
    nt = seq // lt
    width = pairs * LANES
    items = pairs * (lt // RW_CHUNK)
    pair = 2 * RW_CHUNK
    blk_in = pl.BlockSpec((lt, width), lambda bi, hp, ti: (bi * nt + jnp.minimum(ti, nt - 1), hp))
    blk_out = pl.BlockSpec((lt, width), lambda bi, hp, ti: (bi * nt + jnp.maximum(ti - 1, 0), hp))
    return pl.pallas_call(
        functools.partial(_rwscan_kernel, chunks=lt // RW_CHUNK),
        grid=(batch, d // width, nt + 1),
        in_specs=[blk_in] * 7 + [pl.BlockSpec((8, width), lambda bi, hp, ti: (0, hp))],
        out_specs=blk_out,
        out_shape=jax.ShapeDtypeStruct((t, d), BF16),
        scratch_shapes=[
            pltpu.VMEM((pairs, LANES, LANES), F32),
            pltpu.VMEM((2, items, pair, LANES), BF16),
            pltpu.VMEM((2, items, pair, LANES), F32),
            pltpu.VMEM((2, items, LANES, LANES), BF16),
            pltpu.VMEM((2, items, LANES, LANES), F32),
            pltpu.VMEM((2, items, 8, LANES), F32),
            pltpu.VMEM((2, lt, width), F32),
            pltpu.VMEM((2, lt, width), F32),
        ],
        compiler_params=_params("parallel", "parallel", "arbitrary"),
        name="rwscan",
    )(r, lw, k, v, a, b, g, vecs)


def _rope_kernel(pos_ref, inv_ref, cos_ref, sin_ref):
    ang = pos_ref[...].astype(F32) * inv_ref[...]
    cos_ref[...] = jnp.cos(ang)
    sin_ref[...] = jnp.sin(ang)


def _rope_tables(pos, inv_freq, tm=512):
    t = pos.shape[0]
    half = inv_freq.shape[1]
    out = jax.ShapeDtypeStruct((t, half), F32)
    return pl.pallas_call(
        _rope_kernel,
        grid=(t // tm,),
        in_specs=[pl.BlockSpec((tm, 1), lambda i: (i, 0)), pl.BlockSpec((1, half), lambda i: (0, 0))],
        out_specs=[pl.BlockSpec((tm, half), lambda i: (i, 0))] * 2,
        out_shape=[out, out],
        compiler_params=_params("parallel"),
        name="rope",
    )(pos, inv_freq)


RET_HEADS_PER_STEP = 2


def _retscan_kernel(q_ref, k_ref, v_ref, gate_ref, cos_ref, sin_ref, lg_ref, gn_ref, o_ref, st_ref, intra_ref):
    lc = q_ref.shape[0]
    heads = lg_ref.shape[0]
    dk = q_ref.shape[1] // heads
    dv = v_ref.shape[1] // heads
    half = dk // 2
    ic = lax.broadcasted_iota(jnp.int32, (lc, 1), 0).astype(F32)
    lgs = [lg_ref[h][:, 0:1] for h in range(heads)]

    @pl.when(pl.program_id(2) == 0)
    def _():
        st_ref[...] = jnp.zeros_like(st_ref)
        diff = ic - lax.broadcasted_iota(jnp.int32, (1, lc), 1).astype(F32)
        causal = diff >= 0
        for h in range(heads):
            intra_ref[h] = jnp.where(causal, jnp.exp(lgs[h] * jnp.where(causal, diff, 0.0)), 0.0)

    cos, sin = cos_ref[...], sin_ref[...]

    def rot(t):
        t1, t2 = t[:, :half], t[:, half:]
        return jnp.concatenate([t1 * cos - t2 * sin, t1 * sin + t2 * cos], axis=1)

    for h in range(heads):
        lg = lgs[h]
        q = rot(q_ref[:, h * dk:(h + 1) * dk].astype(F32))
        k = rot(k_ref[:, h * dk:(h + 1) * dk].astype(F32)) * (dk ** -0.5)
        v = v_ref[:, h * dv:(h + 1) * dv]
        xi = jnp.exp(lg * (ic + 1.0))
        zeta = jnp.exp(lg * (lc - 1.0 - ic))
        st = st_ref[h]
        sc = _mm_nt(q, k) * intra_ref[h]
        y = _mm(sc, v) + _mm(q, st) * xi
        st_ref[h] = jnp.exp(lg * lc) * st + _mm_tn(k * zeta, v)
        mean = jnp.mean(y, axis=-1, keepdims=True)
        yc = y - mean
        yn = yc * lax.rsqrt(jnp.mean(yc * yc, axis=-1, keepdims=True) + NORM_EPS) * gn_ref[:, h * dv:(h + 1) * dv]
        gate = gate_ref[:, h * dv:(h + 1) * dv].astype(F32)
        o_ref[:, h * dv:(h + 1) * dv] = (gate * _sigmoid(gate) * yn).astype(BF16)


def _retscan(proj, cos, sin, lg, gn_w, batch, seq, d, lc=256):
    t = proj.shape[0]
    nc = seq // lc
    hps = RET_HEADS_PER_STEP
    nhb = RET_HEADS // hps
    dk, dv = d // RET_HEADS, 2 * d // RET_HEADS
    tok = lambda bi, hb, ci: bi * nc + ci
    return pl.pallas_call(
        _retscan_kernel,
        grid=(batch, nhb, nc),
        in_specs=[
            pl.BlockSpec((lc, hps * dk), lambda bi, hb, ci: (tok(bi, hb, ci), hb)),
            pl.BlockSpec((lc, hps * dk), lambda bi, hb, ci: (tok(bi, hb, ci), nhb + hb)),
            pl.BlockSpec((lc, hps * dv), lambda bi, hb, ci: (tok(bi, hb, ci), nhb + hb)),
            pl.BlockSpec((lc, hps * dv), lambda bi, hb, ci: (tok(bi, hb, ci), 2 * nhb + hb)),
            pl.BlockSpec((lc, dk // 2), lambda bi, hb, ci: (tok(bi, hb, ci), 0)),
            pl.BlockSpec((lc, dk // 2), lambda bi, hb, ci: (tok(bi, hb, ci), 0)),
            pl.BlockSpec((hps, 1, LANES), lambda bi, hb, ci: (hb, 0, 0)),
            pl.BlockSpec((1, hps * dv), lambda bi, hb, ci: (0, hb)),
        ],
        out_specs=pl.BlockSpec((lc, hps * dv), lambda bi, hb, ci: (tok(bi, hb, ci), hb)),
        out_shape=jax.ShapeDtypeStruct((t, 2 * d), BF16),
        scratch_shapes=[pltpu.VMEM((hps, dk, dv), F32), pltpu.VMEM((hps, lc, lc), F32)],
        compiler_params=_params("parallel", "parallel", "arbitrary"),
        name="retscan",
    )(proj, proj, proj, proj, cos, sin, lg, gn_w)


def _mlscan_kernel(q_ref, k_ref, v_ref, og_ref, gt_ref, bif_ref, hn_ref, o_ref, c_ref, m_ref):
    lc = q_ref.shape[0]
    dqk = q_ref.shape[1] // ML_HEADS
    dv = v_ref.shape[1] // ML_HEADS

    @pl.when(pl.program_id(1) == 0)
    def _():
        c_ref[...] = jnp.zeros_like(c_ref)
        m_ref[...] = jnp.zeros_like(m_ref)

    gt = gt_ref[...] + bif_ref[...]
    gt = ML_GATE_CAP * jnp.tanh(gt * (1.0 / ML_GATE_CAP))
    lane = lax.broadcasted_iota(jnp.int32, gt.shape, 1)
    is_f = (lane >= ML_HEADS) & (lane < 2 * ML_HEADS)
    log_f = jnp.where(is_f, -_softplus(-gt), 0.0)
    bcum = pltpu.roll(_cumsum_rows(log_f), LANES - ML_HEADS, 1)
    rj_t = (bcum - gt).T
    ir = lax.broadcasted_iota(jnp.int32, (lc, lc), 0)
    ic = lax.broadcasted_iota(jnp.int32, (lc, lc), 1)
    causal = ir >= ic
    ones_col = jnp.where(lax.broadcasted_iota(jnp.int32, (lc, LANES), 1) == 0, 1.0, 0.0).astype(BF16)
    for h in range(ML_HEADS):
        q = q_ref[:, h * dqk:(h + 1) * dqk]
        k = k_ref[:, h * dqk:(h + 1) * dqk].astype(F32) * (dqk ** -0.5)
        vext = jnp.concatenate([v_ref[:, h * dv:(h + 1) * dv], ones_col], axis=1)
        b_col = bcum[:, h:h + 1]
        i_col = gt[:, h:h + 1]
        m_st = m_ref[h:h + 1, 0:1]
        log_d = jnp.where(causal, b_col - rj_t[h:h + 1, :], -jnp.inf)
        log_inter = b_col + m_st
        m_t = jnp.maximum(log_inter, jnp.max(log_d, axis=-1, keepdims=True))
        dmat = jnp.exp(log_d - m_t)
        w_inter = jnp.exp(log_inter - m_t)
        sc = _mm_nt(q, k) * dmat
        c_st = c_ref[h]
        num = _mm(sc, vext) + w_inter * _mm(q, c_st)
        dot = num[:, dv:dv + 1]
        hc = num[:, :dv] / jnp.maximum(jnp.abs(dot), jnp.exp(-m_t))
        m_new = m_t[lc - 1:lc, :]
        b_last = b_col[lc - 1:lc, :]
        w_s = jnp.exp(b_last - b_col + i_col - m_new)
        dec = jnp.exp(b_last + m_st - m_new)
        c_ref[h] = dec * c_st + _mm_tn(k * w_s, vext)
        m_ref[h:h + 1, :] = jnp.broadcast_to(m_new, (1, LANES))
        yf = hc * lax.rsqrt(jnp.mean(hc * hc, axis=-1, keepdims=True) + NORM_EPS)
        og = og_ref[:, h * dv:(h + 1) * dv].astype(F32)
        o_ref[:, h * dv:(h + 1) * dv] = (_sigmoid(og) * (yf * hn_ref[:, h * dv:(h + 1) * dv])).astype(BF16)


def _mlscan(proj, gates, bif, hn_w, batch, seq, d, lc=256):
    t = proj.shape[0]
    nc = seq // lc
    dq = d // 2
    tok = lambda bi, ci: bi * nc + ci
    return pl.pallas_call(
        _mlscan_kernel,
        grid=(batch, nc),
        in_specs=[
            pl.BlockSpec((lc, dq), lambda bi, ci: (tok(bi, ci), 0)),
            pl.BlockSpec((lc, dq), lambda bi, ci: (tok(bi, ci), 1)),
            pl.BlockSpec((lc, d), lambda bi, ci: (tok(bi, ci), 1)),
            pl.BlockSpec((lc, d), lambda bi, ci: (tok(bi, ci), 2)),
            pl.BlockSpec((lc, LANES), lambda bi, ci: (tok(bi, ci), 0)),
            pl.BlockSpec((1, LANES), lambda bi, ci: (0, 0)),
            pl.BlockSpec((1, d), lambda bi, ci: (0, 0)),
        ],
        out_specs=pl.BlockSpec((lc, d), lambda bi, ci: (tok(bi, ci), 0)),
        out_shape=jax.ShapeDtypeStruct((t, d), BF16),
        scratch_shapes=[pltpu.VMEM((ML_HEADS, dq // ML_HEADS, d // ML_HEADS + LANES), F32),
                        pltpu.VMEM((8, LANES), F32)],
        compiler_params=_params("parallel", "arbitrary"),
        name="mlscan",
    )(proj, proj, proj, proj, gates, bif, hn_w)


def _pad_cols(w, n):
    return jnp.pad(w, ((0, 0), (0, n - w.shape[1])))


def _pad_rows(w, n):
    return jnp.pad(w, ((0, n - w.shape[0]), (0, 0)))


def kernel(x, positions, norm_mix, norm_ffn, norm_final, rw_mu, rw_w_rkv, rw_w0, rw_w1, rw_w2, rw_a0, rw_a1, rw_a2, rw_g1, rw_g2, rw_k_k, rw_k_a, rw_r_k, rw_ln_w, rw_ln_b, rw_w_o, rw_v0, rw_v1, rw_v2, ret_w_in, ret_gn_w, ret_w_out, ml_w_in, ml_b_if, ml_hn_w, ml_w_out, ffn_w1, ffn_w3, ffn_w2):
    batch, seq, d = x.shape
    t = batch * seq
    depth = norm_mix.shape[0]
    xf = x.reshape(t, d)
    row = lambda vec: vec.reshape(1, -1).astype(F32)

    dk = d // RET_HEADS
    inv_freq = (1.0 / (ROPE_BASE ** jnp.linspace(0.0, 1.0, dk // 2, dtype=F32))).reshape(1, -1)
    cos, sin = _rope_tables(positions.reshape(t, 1), inv_freq)
    log_gamma = jnp.log1p(-jnp.exp2(-5.0 - jnp.arange(RET_HEADS, dtype=F32)))
    lg = jnp.broadcast_to(log_gamma[:, None, None], (RET_HEADS, 1, LANES))

    v_first = None
    for i in range(depth):
        kind, j = i % 3, i // 3
        g_mix = row(norm_mix[i])
        if kind == 0:
            has_vres = j > 0
            zeros_v1 = jnp.zeros((d, RW_LORA_G - RW_LORA_V), F32)
            zeros_v2 = jnp.zeros((RW_LORA_G - RW_LORA_V, d), F32)
            lin = jnp.concatenate([
                _pad_cols(rw_w1[j], RW_LORA_A - RW_LORA_W), _pad_cols(rw_a1[j], RW_LORA_V - RW_LORA_A),
                _pad_cols(rw_v1[j - 1], RW_LORA_G - RW_LORA_V) if has_vres else zeros_v1,
                rw_g1[j]], axis=1).astype(BF16)
            lout = jnp.concatenate([
                _pad_rows(rw_w2[j], RW_LORA_A - RW_LORA_W), _pad_rows(rw_a2[j], RW_LORA_V - RW_LORA_A),
                _pad_rows(rw_v2[j - 1], RW_LORA_G - RW_LORA_V) if has_vres else zeros_v2,
                rw_g2[j]], axis=0).astype(BF16)
            v0 = rw_v0[j - 1] if has_vres else jnp.zeros((d,), F32)
            zero = jnp.zeros((d,), F32)
            vecs = jnp.stack([rw_w0[j], rw_a0[j], v0, rw_k_k[j], rw_k_a[j], zero, zero, zero]).astype(F32)
            r, lw, k, v, a, b, g = _rwproj(xf, g_mix, rw_mu[j].astype(F32), rw_w_rkv[j].astype(BF16), lin, lout,
                                           vecs, v_first if has_vres else None, seq)
            if j == 0:
                v_first = v
            svecs = jnp.stack([rw_r_k[j].reshape(-1), rw_ln_w[j], rw_ln_b[j], zero, zero, zero, zero, zero]).astype(F32)
            z = _rwscan(r, lw, k, v, a, b, g, svecs, batch, seq)
            xf = _outproj(z, rw_w_o[j].astype(BF16), xf)
        elif kind == 1:
            (proj,) = _normproj(xf, g_mix, ret_w_in[j].astype(BF16), tn=1024)
            z = _retscan(proj, cos, sin, lg, row(ret_gn_w[j]), batch, seq, d)
            xf = _outproj(z, ret_w_out[j].astype(BF16), xf)
        else:
            n_pad = 3 * d + 2 * LANES
            tn = n_pad // 5
            w_in = _pad_cols(ml_w_in[j], n_pad).astype(BF16)
            gate_lo = 3 * d - (n_pad - tn)
            proj, gates = _normproj(xf, g_mix, w_in, tn=tn, side_cols=(gate_lo, gate_lo + LANES))
            bif = _pad_cols(ml_b_if[j].reshape(1, -1).astype(F32), LANES)
            z = _mlscan(proj, gates, bif, row(ml_hn_w[j]), batch, seq, d)
            xf = _outproj(z, ml_w_out[j].astype(BF16), xf)
        xf = _ffn(xf, row(norm_ffn[i]), ffn_w1[i].astype(BF16), ffn_w3[i].astype(BF16), ffn_w2[i].astype(BF16),
                  row(norm_final), final_norm=(i == depth - 1))
    return xf.reshape(batch, seq, d)
```

```python
import functools

import jax
import jax.numpy as jnp
from jax import lax
from jax.experimental import pallas as pl
from jax.experimental.pallas import tpu as pltpu

F32 = jnp.float32
BF16 = jnp.bfloat16

NORM_EPS = 1e-6
RW_HEAD = 64
RW_GN_EPS = 64e-5
RW_CHUNK = 64
RET_HEADS = 8
ROPE_BASE = 10000.0
ML_HEADS = 4
ML_GATE_CAP = 15.0
LANES = 128
NORM_ROWS = 256
VMEM_LIMIT = 56 * 1024 * 1024


def _params(*sem):
    return pltpu.CompilerParams(dimension_semantics=sem, vmem_limit_bytes=VMEM_LIMIT)


def _mm(a, b):
    return jnp.dot(a.astype(BF16), b.astype(BF16), preferred_element_type=F32)


def _mm_nt(a, b):
    return lax.dot_general(a.astype(BF16), b.astype(BF16), (((1,), (1,)), ((), ())),
                           preferred_element_type=F32)


def _mm_tn(a, b):
    return lax.dot_general(a.astype(BF16), b.astype(BF16), (((0,), (0,)), ((), ())),
                           preferred_element_type=F32)


def _rms(x, g):
    return x * lax.rsqrt(jnp.mean(x * x, axis=-1, keepdims=True) + NORM_EPS) * g


def _sigmoid(x):
    return 0.5 * jnp.tanh(0.5 * x) + 0.5


def _softplus(x):
    return jnp.maximum(x, 0.0) + jnp.log(1.0 + jnp.exp(-jnp.abs(x)))


def _cumsum_rows(x):
    n = x.shape[0]
    row = lax.broadcasted_iota(jnp.int32, x.shape, 0)
    s = 1
    while s < n:
        x = x + jnp.where(row >= s, pltpu.roll(x, s, 0), 0.0)
        s *= 2
    return x


def _seg64_sum(x, split=True):
    r = lax.broadcasted_iota(jnp.int32, (LANES, LANES), 0) // RW_HEAD
    c = lax.broadcasted_iota(jnp.int32, (LANES, LANES), 1) // RW_HEAD
    bd = jnp.where(r == c, 1.0, 0.0).astype(BF16)
    hi = x.astype(BF16)
    out = jnp.dot(hi, bd, preferred_element_type=F32)
    if split:
        out = out + jnp.dot((x - hi.astype(F32)).astype(BF16), bd, preferred_element_type=F32)
    return out


def _ffn_kernel(x_ref, g_ref, w1_ref, w3_ref, w2_ref, gf_ref, o_ref, h_s, *, final_norm):
    j = pl.program_id(1)
    tm = x_ref.shape[0]
    half = w1_ref.shape[1] // 2

    @pl.when(j == 0)
    def _():
        for s in range(0, tm, NORM_ROWS):
            x = x_ref[s:s + NORM_ROWS, :]
            h_s[s:s + NORM_ROWS, :] = _rms(x, g_ref[...]).astype(BF16)
            o_ref[s:s + NORM_ROWS, :] = x

    h = h_s[...]
    gate_up = [(jnp.dot(h, w1_ref[:, c:c + half], preferred_element_type=F32),
                jnp.dot(h, w3_ref[:, c:c + half], preferred_element_type=F32)) for c in (0, half)]
    down = None
    for n, (a, b) in enumerate(gate_up):
        act = (a * _sigmoid(a) * b).astype(BF16)
        part = jnp.dot(act, w2_ref[n * half:(n + 1) * half, :], preferred_element_type=F32)
        down = part if down is None else down + part
    o_ref[...] += down

    if final_norm:
        @pl.when(j == pl.num_programs(1) - 1)
        def _():
            for s in range(0, tm, NORM_ROWS):
                o_ref[s:s + NORM_ROWS, :] = _rms(o_ref[s:s + NORM_ROWS, :], gf_ref[...])


def _ffn(x, g, w1, w3, w2, gf, layer, final_norm, tm=1024, tf=512):
    t, d = x.shape
    ff = w1.shape[2]
    tf = min(tf, ff)
    return pl.pallas_call(
        functools.partial(_ffn_kernel, final_norm=final_norm),
        grid=(t // tm, ff // tf),
        in_specs=[
            pl.BlockSpec((tm, d), lambda i, j: (i, 0), pipeline_mode=pl.Buffered(1)),
            pl.BlockSpec((1, d), lambda i, j: (0, 0)),
            pl.BlockSpec((None, d, tf), lambda i, j: (layer, 0, j)),
            pl.BlockSpec((None, d, tf), lambda i, j: (layer, 0, j)),
            pl.BlockSpec((None, tf, d), lambda i, j: (layer, j, 0)),
            pl.BlockSpec((1, d), lambda i, j: (0, 0)),
        ],
        out_specs=pl.BlockSpec((tm, d), lambda i, j: (i, 0)),
        out_shape=jax.ShapeDtypeStruct((t, d), F32),
        scratch_shapes=[pltpu.VMEM((tm, d), BF16)],
        compiler_params=_params("parallel", "arbitrary"),
        name="ffn",
    )(x, g, w1, w3, w2, gf)


def _outproj_kernel(z_ref, w_ref, x_ref, o_ref):
    o_ref[...] = x_ref[...] + jnp.dot(z_ref[...], w_ref[...], preferred_element_type=F32)


OUTPROJ_W_BYTES = 8 * 1024 * 1024


def _outproj(z, w, x, tm=512):
    t, k = z.shape
    d = w.shape[1]
    tn = min(d, OUTPROJ_W_BYTES // (2 * k))
    return pl.pallas_call(
        _outproj_kernel,
        grid=(d // tn, t // tm),
        in_specs=[
            pl.BlockSpec((tm, k), lambda j, i: (i, 0)),
            pl.BlockSpec((k, tn), lambda j, i: (0, j)),
            pl.BlockSpec((tm, tn), lambda j, i: (i, j)),
        ],
        out_specs=pl.BlockSpec((tm, tn), lambda j, i: (i, j)),
        out_shape=jax.ShapeDtypeStruct((t, d), F32),
        compiler_params=_params("arbitrary", "arbitrary"),
        name="outproj",
    )(z, w, x)


def _normproj_kernel(x_ref, g_ref, w_ref, o_ref, *rest, side_cols):
    h_s = rest[-1]

    @pl.when(pl.program_id(1) == 0)
    def _():
        for s in range(0, x_ref.shape[0], NORM_ROWS):
            h_s[s:s + NORM_ROWS, :] = _rms(x_ref[s:s + NORM_ROWS, :], g_ref[...]).astype(BF16)

    acc = jnp.dot(h_s[...], w_ref[...], preferred_element_type=F32)
    o_ref[...] = acc.astype(o_ref.dtype)
    if side_cols:
        @pl.when(pl.program_id(1) == pl.num_programs(1) - 1)
        def _():
            rest[0][...] = acc[:, side_cols[0]:side_cols[1]]


def _normproj(x, g, w, tn, side_cols=None, tm=1024):
    t, d = x.shape
    n = w.shape[1]
    out_specs = [pl.BlockSpec((tm, tn), lambda i, j: (i, j))]
    out_shape = [jax.ShapeDtypeStruct((t, n), BF16)]
    if side_cols:
        width = side_cols[1] - side_cols[0]
        out_specs.append(pl.BlockSpec((tm, width), lambda i, j: (i, 0)))
        out_shape.append(jax.ShapeDtypeStruct((t, width), F32))
    return pl.pallas_call(
        functools.partial(_normproj_kernel, side_cols=side_cols),
        grid=(t // tm, n // tn),
        in_specs=[
            pl.BlockSpec((tm, d), lambda i, j: (i, 0)),
            pl.BlockSpec((1, d), lambda i, j: (0, 0)),
            pl.BlockSpec((d, tn), lambda i, j: (0, j)),
        ],
        out_specs=out_specs,
        out_shape=out_shape,
        scratch_shapes=[pltpu.VMEM((tm, d), BF16)],
        compiler_params=_params("parallel", "arbitrary"),
        name="normproj",
    )(x, g, w)


RW_LORA_W = 0
RW_LORA_A = 128
RW_LORA_V = 256
RW_LORA_G = 384
RW_LORA_END = 640
RW_PROLOGUE_ROWS = 128


def _rwproj_kernel(*refs, tiles_per_seq, has_vres):
    if has_vres:
        (x_ref, xp_ref, g_ref, mu_ref, wrkv_ref, lin_ref, lout_ref, vec_ref, vf_ref,
         r_o, lw_o, k_o, v_o, a_o, b_o, g_o, xr_s, xk_s, xv_s, l1_s) = refs
    else:
        (x_ref, xp_ref, g_ref, mu_ref, wrkv_ref, lin_ref, lout_ref, vec_ref,
         r_o, lw_o, k_o, v_o, a_o, b_o, g_o, xr_s, xk_s, xv_s, l1_s) = refs
        vf_ref = None
    i = pl.program_id(0)

    @pl.when(pl.program_id(1) == 0)
    def _():
        g = g_ref[...]
        mu = mu_ref[...]
        sub = RW_PROLOGUE_ROWS
        for s in range(x_ref.shape[0] // sub):
            rows = slice(s * sub, (s + 1) * sub)
            h = _rms(x_ref[rows, :], g)
            if s == 0:
                hp = _rms(xp_ref[7:8, :], g)
                hp = jnp.where(i % tiles_per_seq == 0, 0.0, hp)
            else:
                hp = _rms(x_ref[s * sub - 8:s * sub, :], g)[7:8, :]
            row = lax.broadcasted_iota(jnp.int32, h.shape, 0)
            xx = jnp.where(row == 0, hp, pltpu.roll(h, 1, 0)) - h
            xr_s[rows, :] = (h + xx * mu[0:1]).astype(BF16)
            xk_s[rows, :] = (h + xx * mu[2:3]).astype(BF16)
            xv = (h + xx * mu[3:4]).astype(BF16)
            xv_s[rows, :] = xv
            xw = h + xx * mu[1:2]
            xa = h + xx * mu[4:5]
            xg = h + xx * mu[5:6]
            l1_s[rows, RW_LORA_W:RW_LORA_A] = jnp.tanh(_mm(xw, lin_ref[:, RW_LORA_W:RW_LORA_A])).astype(BF16)
            l1_s[rows, RW_LORA_A:RW_LORA_V] = _mm(xa, lin_ref[:, RW_LORA_A:RW_LORA_V]).astype(BF16)
            l1_s[rows, RW_LORA_V:RW_LORA_G] = _mm(xv, lin_ref[:, RW_LORA_V:RW_LORA_G]).astype(BF16)
            l1_s[rows, RW_LORA_G:RW_LORA_END] = _sigmoid(
                _mm(xg, lin_ref[:, RW_LORA_G:RW_LORA_END])).astype(BF16)

    r = jnp.dot(xr_s[...], wrkv_ref[0], preferred_element_type=F32)
    k = jnp.dot(xk_s[...], wrkv_ref[1], preferred_element_type=F32)
    v = jnp.dot(xv_s[...], wrkv_ref[2], preferred_element_type=F32)
    vec = vec_ref[...]
    w0, a0, v0, k_k, k_a = (vec[n:n + 1] for n in range(5))

    def lora2(lo, hi):
        return jnp.dot(l1_s[:, lo:hi], lout_ref[lo:hi, :], preferred_element_type=F32)

    w_log = -_softplus(-(w0 + lora2(RW_LORA_W, RW_LORA_A))) - 0.5
    lw_o[...] = -jnp.exp(w_log)
    a = _sigmoid(a0 + lora2(RW_LORA_A, RW_LORA_V))
    if has_vres:
        v = v + (vf_ref[...].astype(F32) - v) * _sigmoid(v0 + lora2(RW_LORA_V, RW_LORA_G))
    g_o[...] = lora2(RW_LORA_G, RW_LORA_END).astype(g_o.dtype)
    kk = k * k_k
    tn = kk.shape[1]
    ss = jnp.concatenate([_seg64_sum(jnp.square(kk[:, c:c + LANES]), split=False) for c in range(0, tn, LANES)],
                         axis=1)
    kk = kk * lax.rsqrt(jnp.maximum(ss, 1e-24))
    r_o[...] = r.astype(r_o.dtype)
    k_o[...] = (k * (1.0 + (a - 1.0) * k_a)).astype(k_o.dtype)
    v_o[...] = v.astype(v_o.dtype)
    a_o[...] = (-kk).astype(a_o.dtype)
    b_o[...] = (kk * a).astype(b_o.dtype)


def _rwproj(x, g, mu, wrkv, layer, lin, lout, vecs, vfirst, seq, tm=512, tn=512):
    t, d = x.shape
    has_vres = vfirst is not None
    rows8 = tm // 8
    in_specs = [
        pl.BlockSpec((tm, d), lambda i, j: (i, 0)),
        pl.BlockSpec((8, d), lambda i, j: (jnp.maximum(i * rows8 - 1, 0), 0)),
        pl.BlockSpec((1, d), lambda i, j: (0, 0)),
        pl.BlockSpec((6, d), lambda i, j: (0, 0)),
        pl.BlockSpec((None, 3, d, tn), lambda i, j: (layer, 0, 0, j)),
        pl.BlockSpec((d, RW_LORA_END), lambda i, j: (0, 0)),
        pl.BlockSpec((RW_LORA_END, tn), lambda i, j: (0, j)),
        pl.BlockSpec((8, tn), lambda i, j: (0, j)),
    ]
    args = [x, x, g, mu, wrkv, lin, lout, vecs]
    if has_vres:
        in_specs.append(pl.BlockSpec((tm, tn), lambda i, j: (i, j)))
        args.append(vfirst)
    out_dtypes = [BF16, F32, BF16, BF16, BF16, BF16, BF16]
    return pl.pallas_call(
        functools.partial(_rwproj_kernel, tiles_per_seq=seq // tm, has_vres=has_vres),
        grid=(t // tm, d // tn),
        in_specs=in_specs,
        out_specs=[pl.BlockSpec((tm, tn), lambda i, j: (i, j))] * 7,
        out_shape=[jax.ShapeDtypeStruct((t, d), dt) for dt in out_dtypes],
        scratch_shapes=[pltpu.VMEM((tm, d), BF16)] * 3 + [pltpu.VMEM((tm, RW_LORA_END), BF16)],
        compiler_params=_params("parallel", "arbitrary"),
        name="rwproj",
    )(*args)


def _rwscan_kernel(r_ref, lw_ref, k_ref, v_ref, a_ref, b_ref, g_ref, vec_ref, o_ref,
                   s_ref, rp_st, yp_st, mt_st, ct_st, pl_st, bg_st, g_st, *, chunks):
    cl = RW_CHUNK
    pair = 2 * cl
    lt = chunks * cl
    width = lw_ref.shape[1]
    pairs = width // LANES
    ti = pl.program_id(2)
    slot_w = ti % 2
    slot_r = 1 - slot_w

    @pl.when(ti == 0)
    def _():
        s_ref[...] = jnp.zeros_like(s_ref)
        for ref in (rp_st, yp_st, mt_st, ct_st, pl_st, bg_st, g_st):
            ref[1] = jnp.zeros(ref.shape[1:], ref.dtype)

    head0 = lax.broadcasted_iota(jnp.int32, (cl, LANES), 1) < RW_HEAD

    def stack(x):
        return jnp.concatenate([jnp.where(head0, x, 0.0), jnp.where(head0, 0.0, x)], axis=0).astype(BF16)

    vec = vec_ref[...]
    states = [s_ref[p] for p in range(pairs)]
    ys = [[] for _ in range(pairs)]
    links_done = [0]

    def serial_link():
        j = links_done[0]
        if j >= chunks:
            return
        links_done[0] = j + 1
        for p in range(pairs):
            c = p * chunks + j
            st = states[p]
            ys[p].append(_mm_nt(rp_st[slot_r, c], stack(st)) + yp_st[slot_r, c])
            states[p] = jnp.exp(pl_st[slot_r, c][0:1]) * st + _mm(st, mt_st[slot_r, c]) + ct_st[slot_r, c]

    norm = {}

    def finish_mean():
        while links_done[0] < chunks:
            serial_link()
        for p in range(pairs):
            s_ref[p] = states[p]
        norm["y"] = [jnp.concatenate(ys[p], axis=0) for p in range(pairs)]
        norm["mean"] = [_seg64_sum(y) * (1.0 / RW_HEAD) for y in norm["y"]]

    def finish_var():
        norm["yc"] = [norm["y"][p] - norm["mean"][p] for p in range(pairs)]
        norm["var"] = [_seg64_sum(yc * yc, split=False) * (1.0 / RW_HEAD) for yc in norm["yc"]]

    def finish_store():
        for p in range(pairs):
            lanes = slice(p * LANES, (p + 1) * LANES)
            yn = norm["yc"][p] * lax.rsqrt(norm["var"][p] + RW_GN_EPS) * vec[1:2, lanes] + vec[2:3, lanes]
            o_ref[:, lanes] = (yn * g_st[slot_r, :, lanes] + bg_st[slot_r, :, lanes]).astype(BF16)

    serial_link()

    ri = lax.broadcasted_iota(jnp.int32, (cl, LANES), 0)
    ci = lax.broadcasted_iota(jnp.int32, (cl, LANES), 1) % cl
    strict = ri > ci
    incl = ri >= ci
    eye = jnp.where(ri == ci, 1.0, 0.0)
    rb = lax.broadcasted_iota(jnp.int32, (pair, LANES), 0) // cl
    cb = lax.broadcasted_iota(jnp.int32, (pair, LANES), 1) // RW_HEAD
    same_head = rb == cb

    lw = lw_ref[...]
    rowc = lax.broadcasted_iota(jnp.int32, (lt, width), 0) % cl
    cum = lw
    s = 1
    while s < cl:
        cum = cum + jnp.where(rowc >= s, pltpu.roll(cum, s, 0), 0.0)
        s *= 2
    r, k, v = r_ref[...].astype(F32), k_ref[...].astype(F32), v_ref[...].astype(F32)
    a, b = a_ref[...].astype(F32), b_ref[...].astype(F32)
    g = g_ref[...].astype(F32)
    rkr = r * k * vec[0:1]
    bonus = jnp.concatenate([_seg64_sum(rkr[:, p * LANES:(p + 1) * LANES], split=False) for p in range(pairs)],
                            axis=1)
    bg_st[slot_w] = bonus * v * g
    g_st[slot_w] = g
    e_neg = jnp.exp(-cum)
    rt = r * jnp.exp(cum)
    at = a * jnp.exp(cum - lw)
    bt = b * e_neg
    kt = k * e_neg

    cs = range(pairs * chunks)
    rows = [(slice((c % chunks) * cl, (c % chunks + 1) * cl),
             slice((c // chunks) * LANES, (c // chunks + 1) * LANES)) for c in cs]
    last = [cum[rows[c][0], rows[c][1]][cl - 1:cl, :] for c in cs]
    tail = [jnp.exp(last[c] - cum[rows[c]]) for c in cs]
    at_c = [at[rows[c]].astype(BF16) for c in cs]
    rt_c = [rt[rows[c]].astype(BF16) for c in cs]
    v_c = [v[rows[c]].astype(BF16) for c in cs]
    v_s = [stack(v[rows[c]]) for c in cs]
    bl_c = [(b[rows[c]] * tail[c]).astype(BF16) for c in cs]
    kl_c = [(k[rows[c]] * tail[c]).astype(BF16) for c in cs]
    gram = [_mm_nt(jnp.concatenate([at_c[c], rt_c[c]], axis=0),
                   jnp.concatenate([stack(bt[rows[c]]), stack(kt[rows[c]])], axis=0)) for c in cs]
    serial_link()
    n_ab = [jnp.where(strict, gram[c][:cl, :LANES], 0.0) for c in cs]
    a_ak = [jnp.where(strict, gram[c][:cl, LANES:], 0.0) for c in cs]
    a_r = [jnp.concatenate([jnp.where(incl, gram[c][cl:, :LANES], 0.0),
                            jnp.where(incl, gram[c][cl:, LANES:], 0.0)], axis=1).astype(BF16) for c in cs]
    x1 = [_mm(a_ak[c], v_s[c]) for c in cs]
    serial_link()
    tinv = [eye + n_ab[c] for c in cs]
    pw = [_mm(n_ab[c], stack(n_ab[c])) for c in cs]
    serial_link()
    span = 4
    while span < cl:
        both = [_mm(jnp.concatenate([pw[c], tinv[c]], axis=0), stack(pw[c])) for c in cs]
        pw = [both[c][:cl] for c in cs]
        tinv = [tinv[c] + both[c][cl:] for c in cs]
        span *= 2
        serial_link()
    tinv = [tinv[c] + _mm(tinv[c], stack(pw[c])) for c in cs]
    finish_mean()
    tz = [_mm(tinv[c], jnp.concatenate([stack(at[rows[c]]), stack(x1[c])], axis=1)) for c in cs]
    finish_var()
    zeros = jnp.zeros((pair, LANES), BF16)
    q = [_mm(a_r[c], jnp.concatenate(
        [jnp.concatenate([stack(tz[c][:, :LANES]), stack(tz[c][:, LANES:])], axis=1),
         jnp.concatenate([zeros, v_s[c]], axis=1)], axis=0)) for c in cs]
    finish_store()
    mt = [jnp.where(same_head, _mm_tn(tz[c][:, :LANES], bl_c[c]), 0.0) for c in cs]
    ct = [_mm_tn(jnp.concatenate([tz[c][:, LANES:].astype(BF16), v_c[c]], axis=0),
                 jnp.concatenate([bl_c[c], kl_c[c]], axis=0)) for c in cs]

    for c in cs:
        rp_st[slot_w, c] = (rt[rows[c]] + q[c][:, :LANES]).astype(BF16)
        yp_st[slot_w, c] = q[c][:, LANES:]
        mt_st[slot_w, c] = mt[c].astype(BF16)
        ct_st[slot_w, c] = jnp.where(head0, ct[c][:cl], ct[c][cl:])
        pl_st[slot_w, c] = jnp.broadcast_to(last[c], (8, LANES))


def _rwscan(r, lw, k, v, a, b, g, vecs, batch, seq, lt=512, pairs=2):
    t, d = r.shape
    nt = seq // lt
    width = pairs * LANES
    items = pairs * (lt // RW_CHUNK)
    pair = 2 * RW_CHUNK
    blk_in = pl.BlockSpec((lt, width), lambda bi, hp, ti: (bi * nt + jnp.minimum(ti, nt - 1), hp))
    blk_out = pl.BlockSpec((lt, width), lambda bi, hp, ti: (bi * nt + jnp.maximum(ti - 1, 0), hp))
    return pl.pallas_call(
        functools.partial(_rwscan_kernel, chunks=lt // RW_CHUNK),
        grid=(batch, d // width, nt + 1),
        in_specs=[blk_in] * 7 + [pl.BlockSpec((8, width), lambda bi, hp, ti: (0, hp))],
        out_specs=blk_out,
        out_shape=jax.ShapeDtypeStruct((t, d), BF16),
        scratch_shapes=[
            pltpu.VMEM((pairs, RW_HEAD, LANES), F32),
            pltpu.VMEM((2, items, RW_CHUNK, LANES), BF16),
            pltpu.VMEM((2, items, RW_CHUNK, LANES), F32),
            pltpu.VMEM((2, items, LANES, LANES), BF16),
            pltpu.VMEM((2, items, RW_HEAD, LANES), F32),
            pltpu.VMEM((2, items, 8, LANES), F32),
            pltpu.VMEM((2, lt, width), F32),
            pltpu.VMEM((2, lt, width), F32),
        ],
        compiler_params=_params("parallel", "parallel", "arbitrary"),
        name="rwscan",
    )(r, lw, k, v, a, b, g, vecs)


def _rope_kernel(pos_ref, inv_ref, cos_ref, sin_ref):
    ang = pos_ref[...].astype(F32) * inv_ref[...]
    cos_ref[...] = jnp.cos(ang)
    sin_ref[...] = jnp.sin(ang)


def _rope_tables(pos, inv_freq, tm=512):
    t = pos.shape[0]
    half = inv_freq.shape[1]
    out = jax.ShapeDtypeStruct((t, half), F32)
    return pl.pallas_call(
        _rope_kernel,
        grid=(t // tm,),
        in_specs=[pl.BlockSpec((tm, 1), lambda i: (i, 0)), pl.BlockSpec((1, half), lambda i: (0, 0))],
        out_specs=[pl.BlockSpec((tm, half), lambda i: (i, 0))] * 2,
        out_shape=[out, out],
        compiler_params=_params("parallel"),
        name="rope",
    )(pos, inv_freq)


RET_HEADS_PER_STEP = 2


def _retscan_kernel(q_ref, k_ref, v_ref, gate_ref, cos_ref, sin_ref, lg_ref, gn_ref, o_ref, st_ref, intra_ref):
    lc = q_ref.shape[0]
    heads = lg_ref.shape[0]
    dk = q_ref.shape[1] // heads
    dv = v_ref.shape[1] // heads
    half = dk // 2
    ic = lax.broadcasted_iota(jnp.int32, (lc, 1), 0).astype(F32)
    lgs = [lg_ref[h][:, 0:1] for h in range(heads)]

    @pl.when(pl.program_id(2) == 0)
    def _():
        st_ref[...] = jnp.zeros_like(st_ref)
        diff = ic - lax.broadcasted_iota(jnp.int32, (1, lc), 1).astype(F32)
        causal = diff >= 0
        for h in range(heads):
            intra_ref[h] = jnp.where(causal, jnp.exp(lgs[h] * jnp.where(causal, diff, 0.0)), 0.0)

    cos, sin = cos_ref[...], sin_ref[...]

    def rot(t):
        t1, t2 = t[:, :half], t[:, half:]
        return jnp.concatenate([t1 * cos - t2 * sin, t1 * sin + t2 * cos], axis=1)

    for h in range(heads):
        lg = lgs[h]
        q = rot(q_ref[:, h * dk:(h + 1) * dk].astype(F32))
        k = rot(k_ref[:, h * dk:(h + 1) * dk].astype(F32)) * (dk ** -0.5)
        v = v_ref[:, h * dv:(h + 1) * dv]
        xi = jnp.exp(lg * (ic + 1.0))
        zeta = jnp.exp(lg * (lc - 1.0 - ic))
        st = st_ref[h]
        sc = _mm_nt(q, k) * intra_ref[h]
        y = _mm(sc, v) + _mm(q, st) * xi
        st_ref[h] = jnp.exp(lg * lc) * st + _mm_tn(k * zeta, v)
        mean = jnp.mean(y, axis=-1, keepdims=True)
        yc = y - mean
        yn = yc * lax.rsqrt(jnp.mean(yc * yc, axis=-1, keepdims=True) + NORM_EPS) * gn_ref[:, h * dv:(h + 1) * dv]
        gate = gate_ref[:, h * dv:(h + 1) * dv].astype(F32)
        o_ref[:, h * dv:(h + 1) * dv] = (gate * _sigmoid(gate) * yn).astype(BF16)


def _retscan(proj, cos, sin, lg, gn_w, batch, seq, d, lc=256):
    t = proj.shape[0]
    nc = seq // lc
    hps = RET_HEADS_PER_STEP
    nhb = RET_HEADS // hps
    dk, dv = d // RET_HEADS, 2 * d // RET_HEADS
    tok = lambda bi, hb, ci: bi * nc + ci
    return pl.pallas_call(
        _retscan_kernel,
        grid=(batch, nhb, nc),
        in_specs=[
            pl.BlockSpec((lc, hps * dk), lambda bi, hb, ci: (tok(bi, hb, ci), hb)),
            pl.BlockSpec((lc, hps * dk), lambda bi, hb, ci: (tok(bi, hb, ci), nhb + hb)),
            pl.BlockSpec((lc, hps * dv), lambda bi, hb, ci: (tok(bi, hb, ci), nhb + hb)),
            pl.BlockSpec((lc, hps * dv), lambda bi, hb, ci: (tok(bi, hb, ci), 2 * nhb + hb)),
            pl.BlockSpec((lc, dk // 2), lambda bi, hb, ci: (tok(bi, hb, ci), 0)),
            pl.BlockSpec((lc, dk // 2), lambda bi, hb, ci: (tok(bi, hb, ci), 0)),
            pl.BlockSpec((hps, 1, LANES), lambda bi, hb, ci: (hb, 0, 0)),
            pl.BlockSpec((1, hps * dv), lambda bi, hb, ci: (0, hb)),
        ],
        out_specs=pl.BlockSpec((lc, hps * dv), lambda bi, hb, ci: (tok(bi, hb, ci), hb)),
        out_shape=jax.ShapeDtypeStruct((t, 2 * d), BF16),
        scratch_shapes=[pltpu.VMEM((hps, dk, dv), F32), pltpu.VMEM((hps, lc, lc), F32)],
        compiler_params=_params("parallel", "parallel", "arbitrary"),
        name="retscan",
    )(proj, proj, proj, proj, cos, sin, lg, gn_w)


def _mlscan_kernel(q_ref, k_ref, v_ref, og_ref, gt_ref, bif_ref, hn_ref, o_ref, c_ref, m_ref):
    lc = q_ref.shape[0]
    dqk = q_ref.shape[1] // ML_HEADS
    dv = v_ref.shape[1] // ML_HEADS

    @pl.when(pl.program_id(1) == 0)
    def _():
        c_ref[...] = jnp.zeros_like(c_ref)
        m_ref[...] = jnp.zeros_like(m_ref)

    gt = gt_ref[...] + bif_ref[...]
    gt = ML_GATE_CAP * jnp.tanh(gt * (1.0 / ML_GATE_CAP))
    lane = lax.broadcasted_iota(jnp.int32, gt.shape, 1)
    is_f = (lane >= ML_HEADS) & (lane < 2 * ML_HEADS)
    log_f = jnp.where(is_f, -_softplus(-gt), 0.0)
    bcum = pltpu.roll(_cumsum_rows(log_f), LANES - ML_HEADS, 1)
    rj_t = (bcum - gt).T
    ir = lax.broadcasted_iota(jnp.int32, (lc, lc), 0)
    ic = lax.broadcasted_iota(jnp.int32, (lc, lc), 1)
    causal = ir >= ic
    ones_col = jnp.where(lax.broadcasted_iota(jnp.int32, (lc, LANES), 1) == 0, 1.0, 0.0).astype(BF16)
    for h in range(ML_HEADS):
        q = q_ref[:, h * dqk:(h + 1) * dqk]
        k = k_ref[:, h * dqk:(h + 1) * dqk].astype(F32) * (dqk ** -0.5)
        vext = jnp.concatenate([v_ref[:, h * dv:(h + 1) * dv], ones_col], axis=1)
        b_col = bcum[:, h:h + 1]
        i_col = gt[:, h:h + 1]
        m_st = m_ref[h:h + 1, 0:1]
        log_d = jnp.where(causal, b_col - rj_t[h:h + 1, :], -jnp.inf)
        log_inter = b_col + m_st
        m_t = jnp.maximum(log_inter, jnp.max(log_d, axis=-1, keepdims=True))
        dmat = jnp.exp(log_d - m_t)
        w_inter = jnp.exp(log_inter - m_t)
        sc = _mm_nt(q, k) * dmat
        c_st = c_ref[h]
        num = _mm(sc, vext) + w_inter * _mm(q, c_st)
        dot = num[:, dv:dv + 1]
        hc = num[:, :dv] / jnp.maximum(jnp.abs(dot), jnp.exp(-m_t))
        m_new = m_t[lc - 1:lc, :]
        b_last = b_col[lc - 1:lc, :]
        w_s = jnp.exp(b_last - b_col + i_col - m_new)
        dec = jnp.exp(b_last + m_st - m_new)
        c_ref[h] = dec * c_st + _mm_tn(k * w_s, vext)
        m_ref[h:h + 1, :] = jnp.broadcast_to(m_new, (1, LANES))
        yf = hc * lax.rsqrt(jnp.mean(hc * hc, axis=-1, keepdims=True) + NORM_EPS)
        og = og_ref[:, h * dv:(h + 1) * dv].astype(F32)
        o_ref[:, h * dv:(h + 1) * dv] = (_sigmoid(og) * (yf * hn_ref[:, h * dv:(h + 1) * dv])).astype(BF16)


def _mlscan(proj, gates, bif, hn_w, batch, seq, d, lc=256):
    t = proj.shape[0]
    nc = seq // lc
    dq = d // 2
    tok = lambda bi, ci: bi * nc + ci
    return pl.pallas_call(
        _mlscan_kernel,
        grid=(batch, nc),
        in_specs=[
            pl.BlockSpec((lc, dq), lambda bi, ci: (tok(bi, ci), 0)),
            pl.BlockSpec((lc, dq), lambda bi, ci: (tok(bi, ci), 1)),
            pl.BlockSpec((lc, d), lambda bi, ci: (tok(bi, ci), 1)),
            pl.BlockSpec((lc, d), lambda bi, ci: (tok(bi, ci), 2)),
            pl.BlockSpec((lc, LANES), lambda bi, ci: (tok(bi, ci), 0)),
            pl.BlockSpec((1, LANES), lambda bi, ci: (0, 0)),
            pl.BlockSpec((1, d), lambda bi, ci: (0, 0)),
        ],
        out_specs=pl.BlockSpec((lc, d), lambda bi, ci: (tok(bi, ci), 0)),
        out_shape=jax.ShapeDtypeStruct((t, d), BF16),
        scratch_shapes=[pltpu.VMEM((ML_HEADS, dq // ML_HEADS, d // ML_HEADS + LANES), F32),
                        pltpu.VMEM((8, LANES), F32)],
        compiler_params=_params("parallel", "arbitrary"),
        name="mlscan",
    )(proj, proj, proj, proj, gates, bif, hn_w)


def _pad_cols(w, n):
    return jnp.pad(w, ((0, 0), (0, n - w.shape[1])))


def _pad_rows(w, n):
    return jnp.pad(w, ((0, n - w.shape[0]), (0, 0)))


def kernel(x, positions, norm_mix, norm_ffn, norm_final, rw_mu, rw_w_rkv, rw_w0, rw_w1, rw_w2, rw_a0, rw_a1, rw_a2, rw_g1, rw_g2, rw_k_k, rw_k_a, rw_r_k, rw_ln_w, rw_ln_b, rw_w_o, rw_v0, rw_v1, rw_v2, ret_w_in, ret_gn_w, ret_w_out, ml_w_in, ml_b_if, ml_hn_w, ml_w_out, ffn_w1, ffn_w3, ffn_w2):
    batch, seq, d = x.shape
    t = batch * seq
    depth = norm_mix.shape[0]
    xf = x.reshape(t, d)
    row = lambda vec: vec.reshape(1, -1).astype(F32)

    dk = d // RET_HEADS
    inv_freq = (1.0 / (ROPE_BASE ** jnp.linspace(0.0, 1.0, dk // 2, dtype=F32))).reshape(1, -1)
    cos, sin = _rope_tables(positions.reshape(t, 1), inv_freq)
    log_gamma = jnp.log1p(-jnp.exp2(-5.0 - jnp.arange(RET_HEADS, dtype=F32)))
    lg = jnp.broadcast_to(log_gamma[:, None, None], (RET_HEADS, 1, LANES))

    ffn_w1_b, ffn_w3_b, ffn_w2_b = ffn_w1.astype(BF16), ffn_w3.astype(BF16), ffn_w2.astype(BF16)
    rw_w_rkv_b = rw_w_rkv.astype(BF16)

    v_first = None
    for i in range(depth):
        kind, j = i % 3, i // 3
        g_mix = row(norm_mix[i])
        if kind == 0:
            has_vres = j > 0
            zeros_v1 = jnp.zeros((d, RW_LORA_G - RW_LORA_V), F32)
            zeros_v2 = jnp.zeros((RW_LORA_G - RW_LORA_V, d), F32)
            lin = jnp.concatenate([
                _pad_cols(rw_w1[j], RW_LORA_A - RW_LORA_W), _pad_cols(rw_a1[j], RW_LORA_V - RW_LORA_A),
                _pad_cols(rw_v1[j - 1], RW_LORA_G - RW_LORA_V) if has_vres else zeros_v1,
                rw_g1[j]], axis=1).astype(BF16)
            lout = jnp.concatenate([
                _pad_rows(rw_w2[j], RW_LORA_A - RW_LORA_W), _pad_rows(rw_a2[j], RW_LORA_V - RW_LORA_A),
                _pad_rows(rw_v2[j - 1], RW_LORA_G - RW_LORA_V) if has_vres else zeros_v2,
                rw_g2[j]], axis=0).astype(BF16)
            v0 = rw_v0[j - 1] if has_vres else jnp.zeros((d,), F32)
            zero = jnp.zeros((d,), F32)
            vecs = jnp.stack([rw_w0[j], rw_a0[j], v0, rw_k_k[j], rw_k_a[j], zero, zero, zero]).astype(F32)
            r, lw, k, v, a, b, g = _rwproj(xf, g_mix, rw_mu[j].astype(F32), rw_w_rkv_b, j, lin, lout,
                                           vecs, v_first if has_vres else None, seq)
            if j == 0:
                v_first = v
            svecs = jnp.stack([rw_r_k[j].reshape(-1), rw_ln_w[j], rw_ln_b[j], zero, zero, zero, zero, zero]).astype(F32)
            z = _rwscan(r, lw, k, v, a, b, g, svecs, batch, seq)
            xf = _outproj(z, rw_w_o[j].astype(BF16), xf)
        elif kind == 1:
            (proj,) = _normproj(xf, g_mix, ret_w_in[j].astype(BF16), tn=1024)
            z = _retscan(proj, cos, sin, lg, row(ret_gn_w[j]), batch, seq, d)
            xf = _outproj(z, ret_w_out[j].astype(BF16), xf)
        else:
            n_pad = 3 * d + 2 * LANES
            tn = n_pad // 5
            w_in = _pad_cols(ml_w_in[j], n_pad).astype(BF16)
            gate_lo = 3 * d - (n_pad - tn)
            proj, gates = _normproj(xf, g_mix, w_in, tn=tn, side_cols=(gate_lo, gate_lo + LANES))
            bif = _pad_cols(ml_b_if[j].reshape(1, -1).astype(F32), LANES)
            z = _mlscan(proj, gates, bif, row(ml_hn_w[j]), batch, seq, d)
            xf = _outproj(z, ml_w_out[j].astype(BF16), xf)
        xf = _ffn(xf, row(norm_ffn[i]), ffn_w1_b, ffn_w3_b, ffn_w2_b, row(norm_final), layer=i,
                  final_norm=(i == depth - 1))
    return xf.reshape(batch, seq, d)
```

```python
import functools

import jax
import jax.numpy as jnp
from jax import lax
from jax.experimental import pallas as pl
from jax.experimental.pallas import tpu as pltpu

F32 = jnp.float32
BF16 = jnp.bfloat16

NORM_EPS = 1e-6
RW_HEAD = 64
RW_GN_EPS = 64e-5
RW_CHUNK = 64
RET_HEADS = 8
ROPE_BASE = 10000.0
ML_HEADS = 4
ML_GATE_CAP = 15.0
LANES = 128
NORM_ROWS = 256
VMEM_LIMIT = 56 * 1024 * 1024


def _params(*sem):
    return pltpu.CompilerParams(dimension_semantics=sem, vmem_limit_bytes=VMEM_LIMIT)


def _mm(a, b):
    return jnp.dot(a.astype(BF16), b.astype(BF16), preferred_element_type=F32)


def _mm_nt(a, b):
    return lax.dot_general(a.astype(BF16), b.astype(BF16), (((1,), (1,)), ((), ())),
                           preferred_element_type=F32)


def _mm_tn(a, b):
    return lax.dot_general(a.astype(BF16), b.astype(BF16), (((0,), (0,)), ((), ())),
                           preferred_element_type=F32)


def _rms(x, g):
    return x * lax.rsqrt(jnp.mean(x * x, axis=-1, keepdims=True) + NORM_EPS) * g


def _sigmoid(x):
    return 0.5 * jnp.tanh(0.5 * x) + 0.5


def _softplus(x):
    return jnp.maximum(x, 0.0) + jnp.log(1.0 + jnp.exp(-jnp.abs(x)))


def _cumsum_rows(x):
    n = x.shape[0]
    row = lax.broadcasted_iota(jnp.int32, x.shape, 0)
    s = 1
    while s < n:
        x = x + jnp.where(row >= s, pltpu.roll(x, s, 0), 0.0)
        s *= 2
    return x


def _seg64_sum(x, split=True):
    r = lax.broadcasted_iota(jnp.int32, (LANES, LANES), 0) // RW_HEAD
    c = lax.broadcasted_iota(jnp.int32, (LANES, LANES), 1) // RW_HEAD
    bd = jnp.where(r == c, 1.0, 0.0).astype(BF16)
    hi = x.astype(BF16)
    out = jnp.dot(hi, bd, preferred_element_type=F32)
    if split:
        out = out + jnp.dot((x - hi.astype(F32)).astype(BF16), bd, preferred_element_type=F32)
    return out


def _ffn_kernel(*refs, final_norm, cast_next):
    if cast_next:
        (x_ref, g_ref, w1_ref, w3_ref, w2_ref, gf_ref, n1_ref, n3_ref, n2_ref,
         o_ref, n1_o, n3_o, n2_o, h_s) = refs
        n1_o[...] = n1_ref[...].astype(BF16)
        n3_o[...] = n3_ref[...].astype(BF16)
        n2_o[...] = n2_ref[...].astype(BF16)
    else:
        x_ref, g_ref, w1_ref, w3_ref, w2_ref, gf_ref, o_ref, h_s = refs
    j = pl.program_id(1)
    tm = x_ref.shape[0]
    half = w1_ref.shape[1] // 2

    @pl.when(j == 0)
    def _():
        for s in range(0, tm, NORM_ROWS):
            x = x_ref[s:s + NORM_ROWS, :]
            h_s[s:s + NORM_ROWS, :] = _rms(x, g_ref[...]).astype(BF16)
            o_ref[s:s + NORM_ROWS, :] = x

    h = h_s[...]
    gate_up = [(jnp.dot(h, w1_ref[:, c:c + half], preferred_element_type=F32),
                jnp.dot(h, w3_ref[:, c:c + half], preferred_element_type=F32)) for c in (0, half)]
    down = None
    for n, (a, b) in enumerate(gate_up):
        act = (a * _sigmoid(a) * b).astype(BF16)
        part = jnp.dot(act, w2_ref[n * half:(n + 1) * half, :], preferred_element_type=F32)
        down = part if down is None else down + part
    o_ref[...] += down

    if final_norm:
        @pl.when(j == pl.num_programs(1) - 1)
        def _():
            for s in range(0, tm, NORM_ROWS):
                o_ref[s:s + NORM_ROWS, :] = _rms(o_ref[s:s + NORM_ROWS, :], gf_ref[...])


def _ffn(x, g, w1, w3, w2, gf, final_norm, next_weights=None, tm=1024, tf=512):
    t, d = x.shape
    ff = w1.shape[1]
    tf = min(tf, ff)
    ni = t // tm
    in_specs = [
        pl.BlockSpec((tm, d), lambda i, j: (i, 0), pipeline_mode=pl.Buffered(1)),
        pl.BlockSpec((1, d), lambda i, j: (0, 0)),
        pl.BlockSpec((d, tf), lambda i, j: (0, j)),
        pl.BlockSpec((d, tf), lambda i, j: (0, j)),
        pl.BlockSpec((tf, d), lambda i, j: (j, 0)),
        pl.BlockSpec((1, d), lambda i, j: (0, 0)),
    ]
    out_specs = [pl.BlockSpec((tm, d), lambda i, j: (i, 0))]
    out_shape = [jax.ShapeDtypeStruct((t, d), F32)]
    args = [x, g, w1, w3, w2, gf]
    if next_weights is not None:
        w1s, w3s, w2s, layer = next_weights
        rb = d // ni
        assert rb * ni == d and rb % LANES == 0
        in_specs += [pl.BlockSpec((None, rb, tf), lambda i, j: (layer, i, j)),
                     pl.BlockSpec((None, rb, tf), lambda i, j: (layer, i, j)),
                     pl.BlockSpec((None, tf, rb), lambda i, j: (layer, j, i))]
        out_specs += [pl.BlockSpec((rb, tf), lambda i, j: (i, j)),
                      pl.BlockSpec((rb, tf), lambda i, j: (i, j)),
                      pl.BlockSpec((tf, rb), lambda i, j: (j, i))]
        out_shape += [jax.ShapeDtypeStruct((d, ff), BF16), jax.ShapeDtypeStruct((d, ff), BF16),
                      jax.ShapeDtypeStruct((ff, d), BF16)]
        args += [w1s, w3s, w2s]
    return pl.pallas_call(
        functools.partial(_ffn_kernel, final_norm=final_norm, cast_next=next_weights is not None),
        grid=(ni, ff // tf),
        in_specs=in_specs,
        out_specs=out_specs,
        out_shape=out_shape,
        scratch_shapes=[pltpu.VMEM((tm, d), BF16)],
        compiler_params=_params("parallel", "arbitrary"),
        name="ffn",
    )(*args)


def _outproj_kernel(z_ref, w_ref, x_ref, o_ref):
    o_ref[...] = x_ref[...] + jnp.dot(z_ref[...], w_ref[...], preferred_element_type=F32)


OUTPROJ_W_BYTES = 8 * 1024 * 1024


def _outproj(z, w, x, tm=512):
    t, k = z.shape
    d = w.shape[1]
    tn = min(d, OUTPROJ_W_BYTES // (2 * k))
    return pl.pallas_call(
        _outproj_kernel,
        grid=(d // tn, t // tm),
        in_specs=[
            pl.BlockSpec((tm, k), lambda j, i: (i, 0)),
            pl.BlockSpec((k, tn), lambda j, i: (0, j)),
            pl.BlockSpec((tm, tn), lambda j, i: (i, j)),
        ],
        out_specs=pl.BlockSpec((tm, tn), lambda j, i: (i, j)),
        out_shape=jax.ShapeDtypeStruct((t, d), F32),
        compiler_params=_params("arbitrary", "arbitrary"),
        name="outproj",
    )(z, w, x)


def _normproj_kernel(x_ref, g_ref, w_ref, o_ref, *rest, side_cols):
    h_s = rest[-1]

    @pl.when(pl.program_id(1) == 0)
    def _():
        for s in range(0, x_ref.shape[0], NORM_ROWS):
            h_s[s:s + NORM_ROWS, :] = _rms(x_ref[s:s + NORM_ROWS, :], g_ref[...]).astype(BF16)

    acc = jnp.dot(h_s[...], w_ref[...], preferred_element_type=F32)
    o_ref[...] = acc.astype(o_ref.dtype)
    if side_cols:
        @pl.when(pl.program_id(1) == pl.num_programs(1) - 1)
        def _():
            rest[0][...] = acc[:, side_cols[0]:side_cols[1]]


def _normproj(x, g, w, tn, side_cols=None, tm=1024):
    t, d = x.shape
    n = w.shape[1]
    out_specs = [pl.BlockSpec((tm, tn), lambda i, j: (i, j))]
    out_shape = [jax.ShapeDtypeStruct((t, n), BF16)]
    if side_cols:
        width = side_cols[1] - side_cols[0]
        out_specs.append(pl.BlockSpec((tm, width), lambda i, j: (i, 0)))
        out_shape.append(jax.ShapeDtypeStruct((t, width), F32))
    return pl.pallas_call(
        functools.partial(_normproj_kernel, side_cols=side_cols),
        grid=(t // tm, n // tn),
        in_specs=[
            pl.BlockSpec((tm, d), lambda i, j: (i, 0)),
            pl.BlockSpec((1, d), lambda i, j: (0, 0)),
            pl.BlockSpec((d, tn), lambda i, j: (0, j)),
        ],
        out_specs=out_specs,
        out_shape=out_shape,
        scratch_shapes=[pltpu.VMEM((tm, d), BF16)],
        compiler_params=_params("parallel", "arbitrary"),
        name="normproj",
    )(x, g, w)


RW_LORA_W = 0
RW_LORA_A = 128
RW_LORA_V = 256
RW_LORA_G = 384
RW_LORA_END = 640
RW_PROLOGUE_ROWS = 128


def _rwproj_kernel(*refs, tiles_per_seq, has_vres):
    if has_vres:
        (x_ref, xp_ref, g_ref, mu_ref, wrkv_ref, lin_ref, lout_ref, vec_ref, vf_ref,
         r_o, lw_o, k_o, v_o, a_o, b_o, g_o, xr_s, xk_s, xv_s, l1_s) = refs
    else:
        (x_ref, xp_ref, g_ref, mu_ref, wrkv_ref, lin_ref, lout_ref, vec_ref,
         r_o, lw_o, k_o, v_o, a_o, b_o, g_o, xr_s, xk_s, xv_s, l1_s) = refs
        vf_ref = None
    i = pl.program_id(0)

    @pl.when(pl.program_id(1) == 0)
    def _():
        g = g_ref[...]
        mu = mu_ref[...]
        sub = RW_PROLOGUE_ROWS
        for s in range(x_ref.shape[0] // sub):
            rows = slice(s * sub, (s + 1) * sub)
            h = _rms(x_ref[rows, :], g)
            if s == 0:
                hp = _rms(xp_ref[7:8, :], g)
                hp = jnp.where(i % tiles_per_seq == 0, 0.0, hp)
            else:
                hp = _rms(x_ref[s * sub - 8:s * sub, :], g)[7:8, :]
            row = lax.broadcasted_iota(jnp.int32, h.shape, 0)
            xx = jnp.where(row == 0, hp, pltpu.roll(h, 1, 0)) - h
            xr_s[rows, :] = (h + xx * mu[0:1]).astype(BF16)
            xk_s[rows, :] = (h + xx * mu[2:3]).astype(BF16)
            xv = (h + xx * mu[3:4]).astype(BF16)
            xv_s[rows, :] = xv
            xw = h + xx * mu[1:2]
            xa = h + xx * mu[4:5]
            xg = h + xx * mu[5:6]
            l1_s[rows, RW_LORA_W:RW_LORA_A] = jnp.tanh(_mm(xw, lin_ref[:, RW_LORA_W:RW_LORA_A])).astype(BF16)
            l1_s[rows, RW_LORA_A:RW_LORA_V] = _mm(xa, lin_ref[:, RW_LORA_A:RW_LORA_V]).astype(BF16)
            l1_s[rows, RW_LORA_V:RW_LORA_G] = _mm(xv, lin_ref[:, RW_LORA_V:RW_LORA_G]).astype(BF16)
            l1_s[rows, RW_LORA_G:RW_LORA_END] = _sigmoid(
                _mm(xg, lin_ref[:, RW_LORA_G:RW_LORA_END])).astype(BF16)

    r = jnp.dot(xr_s[...], wrkv_ref[0], preferred_element_type=F32)
    k = jnp.dot(xk_s[...], wrkv_ref[1], preferred_element_type=F32)
    v = jnp.dot(xv_s[...], wrkv_ref[2], preferred_element_type=F32)
    vec = vec_ref[...]
    w0, a0, v0, k_k, k_a = (vec[n:n + 1] for n in range(5))

    def lora2(lo, hi):
        return jnp.dot(l1_s[:, lo:hi], lout_ref[lo:hi, :], preferred_element_type=F32)

    w_log = -_softplus(-(w0 + lora2(RW_LORA_W, RW_LORA_A))) - 0.5
    lw_o[...] = -jnp.exp(w_log)
    a = _sigmoid(a0 + lora2(RW_LORA_A, RW_LORA_V))
    if has_vres:
        v = v + (vf_ref[...].astype(F32) - v) * _sigmoid(v0 + lora2(RW_LORA_V, RW_LORA_G))
    g_o[...] = lora2(RW_LORA_G, RW_LORA_END).astype(g_o.dtype)
    kk = k * k_k
    tn = kk.shape[1]
    ss = jnp.concatenate([_seg64_sum(jnp.square(kk[:, c:c + LANES]), split=False) for c in range(0, tn, LANES)],
                         axis=1)
    kk = kk * lax.rsqrt(jnp.maximum(ss, 1e-24))
    r_o[...] = r.astype(r_o.dtype)
    k_o[...] = (k * (1.0 + (a - 1.0) * k_a)).astype(k_o.dtype)
    v_o[...] = v.astype(v_o.dtype)
    a_o[...] = (-kk).astype(a_o.dtype)
    b_o[...] = (kk * a).astype(b_o.dtype)


def _rwproj(x, g, mu, wrkv, layer, lin, lout, vecs, vfirst, seq, tm=512, tn=512):
    t, d = x.shape
    has_vres = vfirst is not None
    rows8 = tm // 8
    in_specs = [
        pl.BlockSpec((tm, d), lambda i, j: (i, 0)),
        pl.BlockSpec((8, d), lambda i, j: (jnp.maximum(i * rows8 - 1, 0), 0)),
        pl.BlockSpec((1, d), lambda i, j: (0, 0)),
        pl.BlockSpec((6, d), lambda i, j: (0, 0)),
        pl.BlockSpec((None, 3, d, tn), lambda i, j: (layer, 0, 0, j)),
        pl.BlockSpec((d, RW_LORA_END), lambda i, j: (0, 0)),
        pl.BlockSpec((RW_LORA_END, tn), lambda i, j: (0, j)),
        pl.BlockSpec((8, tn), lambda i, j: (0, j)),
    ]
    args = [x, x, g, mu, wrkv, lin, lout, vecs]
    if has_vres:
        in_specs.append(pl.BlockSpec((tm, tn), lambda i, j: (i, j)))
        args.append(vfirst)
    out_dtypes = [BF16, F32, BF16, BF16, BF16, BF16, BF16]
    return pl.pallas_call(
        functools.partial(_rwproj_kernel, tiles_per_seq=seq // tm, has_vres=has_vres),
        grid=(t // tm, d // tn),
        in_specs=in_specs,
        out_specs=[pl.BlockSpec((tm, tn), lambda i, j: (i, j))] * 7,
        out_shape=[jax.ShapeDtypeStruct((t, d), dt) for dt in out_dtypes],
        scratch_shapes=[pltpu.VMEM((tm, d), BF16)] * 3 + [pltpu.VMEM((tm, RW_LORA_END), BF16)],
        compiler_params=_params("parallel", "arbitrary"),
        name="rwproj",
    )(*args)


def _rwscan_kernel(r_ref, lw_ref, k_ref, v_ref, a_ref, b_ref, g_ref, vec_ref, o_ref,
                   s_ref, rp_st, yp_st, mt_st, ct_st, pl_st, bg_st, g_st, *, chunks):
    cl = RW_CHUNK
    pair = 2 * cl
    lt = chunks * cl
    width = lw_ref.shape[1]
    pairs = width // LANES
    ti = pl.program_id(2)
    slot_w = ti % 2
    slot_r = 1 - slot_w

    @pl.when(ti == 0)
    def _():
        s_ref[...] = jnp.zeros_like(s_ref)
        for ref in (rp_st, yp_st, mt_st, ct_st, pl_st, bg_st, g_st):
            ref[1] = jnp.zeros(ref.shape[1:], ref.dtype)

    head0 = lax.broadcasted_iota(jnp.int32, (cl, LANES), 1) < RW_HEAD

    def stack(x):
        return jnp.concatenate([jnp.where(head0, x, 0.0), jnp.where(head0, 0.0, x)], axis=0).astype(BF16)

    vec = vec_ref[...]
    states = [s_ref[p] for p in range(pairs)]
    ys = [[] for _ in range(pairs)]
    links_done = [0]

    def serial_link():
        j = links_done[0]
        if j >= chunks:
            return
        links_done[0] = j + 1
        for p in range(pairs):
            c = p * chunks + j
            st = states[p]
            ys[p].append(_mm_nt(rp_st[slot_r, c], stack(st)) + yp_st[slot_r, c])
            states[p] = jnp.exp(pl_st[slot_r, c][0:1]) * st + _mm(st, mt_st[slot_r, c]) + ct_st[slot_r, c]

    norm = {}

    def finish_mean():
        while links_done[0] < chunks:
            serial_link()
        for p in range(pairs):
            s_ref[p] = states[p]
        norm["y"] = [jnp.concatenate(ys[p], axis=0) for p in range(pairs)]
        norm["mean"] = [_seg64_sum(y) * (1.0 / RW_HEAD) for y in norm["y"]]

    def finish_var():
        norm["yc"] = [norm["y"][p] - norm["mean"][p] for p in range(pairs)]
        norm["var"] = [_seg64_sum(yc * yc, split=False) * (1.0 / RW_HEAD) for yc in norm["yc"]]

    def finish_store():
        for p in range(pairs):
            lanes = slice(p * LANES, (p + 1) * LANES)
            yn = norm["yc"][p] * lax.rsqrt(norm["var"][p] + RW_GN_EPS) * vec[1:2, lanes] + vec[2:3, lanes]
            o_ref[:, lanes] = (yn * g_st[slot_r, :, lanes] + bg_st[slot_r, :, lanes]).astype(BF16)

    pending = [finish_mean, finish_var, finish_store]

    def advance():
        if links_done[0] < chunks:
            serial_link()
        elif pending:
            pending.pop(0)()

    advance()

    ri = lax.broadcasted_iota(jnp.int32, (cl, LANES), 0)
    ci = lax.broadcasted_iota(jnp.int32, (cl, LANES), 1) % cl
    strict = ri > ci
    incl = ri >= ci
    eye = jnp.where(ri == ci, 1.0, 0.0)
    rb = lax.broadcasted_iota(jnp.int32, (pair, LANES), 0) // cl
    cb = lax.broadcasted_iota(jnp.int32, (pair, LANES), 1) // RW_HEAD
    same_head = rb == cb

    cs = range(pairs * chunks)
    rows = [(slice((c % chunks) * cl, (c % chunks + 1) * cl),
             slice((c // chunks) * LANES, (c // chunks + 1) * LANES)) for c in cs]

    def prepare(c):
        tok, lanes = rows[c]
        lw = lw_ref[tok, lanes]
        cum = lw
        s = 1
        while s < cl:
            cum = cum + jnp.where(ri >= s, pltpu.roll(cum, s, 0), 0.0)
            s *= 2
        r, k, v = (ref[tok, lanes].astype(F32) for ref in (r_ref, k_ref, v_ref))
        a, b, g = (ref[tok, lanes].astype(F32) for ref in (a_ref, b_ref, g_ref))
        bonus = _seg64_sum(r * k * vec[0:1, lanes], split=False)
        bg_st[slot_w, tok, lanes] = bonus * v * g
        g_st[slot_w, tok, lanes] = g
        last = cum[cl - 1:cl, :]
        e_neg = jnp.exp(-cum)
        tail = jnp.exp(last - cum)
        at = a * jnp.exp(cum - lw)
        rt = r * jnp.exp(cum)
        gram = _mm_nt(jnp.concatenate([at, rt], axis=0),
                      jnp.concatenate([stack(b * e_neg), stack(k * e_neg)], axis=0))
        return dict(at=at, rt=rt, last=last, gram=gram, v_c=v.astype(BF16), v_s=stack(v),
                    bl_c=(b * tail).astype(BF16), kl_c=(k * tail).astype(BF16))

    items = []
    for c in cs:
        items.append(prepare(c))
        if c % 4 == 3:
            advance()
    at, rt, last, gram = ([it[name] for it in items] for name in ("at", "rt", "last", "gram"))
    v_c, v_s, bl_c, kl_c = ([it[name] for it in items] for name in ("v_c", "v_s", "bl_c", "kl_c"))
    n_ab = [jnp.where(strict, gram[c][:cl, :LANES], 0.0) for c in cs]
    a_ak = [jnp.where(strict, gram[c][:cl, LANES:], 0.0) for c in cs]
    a_r = [jnp.concatenate([jnp.where(incl, gram[c][cl:, :LANES], 0.0),
                            jnp.where(incl, gram[c][cl:, LANES:], 0.0)], axis=1).astype(BF16) for c in cs]
    x1 = [_mm(a_ak[c], v_s[c]) for c in cs]
    advance()
    tinv = [eye + n_ab[c] for c in cs]
    pw = [_mm(n_ab[c], stack(n_ab[c])) for c in cs]
    advance()
    span = 4
    while span < cl:
        both = [_mm(jnp.concatenate([pw[c], tinv[c]], axis=0), stack(pw[c])) for c in cs]
        pw = [both[c][:cl] for c in cs]
        tinv = [tinv[c] + both[c][cl:] for c in cs]
        span *= 2
        advance()
    tinv = [tinv[c] + _mm(tinv[c], stack(pw[c])) for c in cs]
    advance()
    tz = [_mm(tinv[c], jnp.concatenate([stack(at[c]), stack(x1[c])], axis=1)) for c in cs]
    advance()
    zeros = jnp.zeros((pair, LANES), BF16)
    q = [_mm(a_r[c], jnp.concatenate(
        [jnp.concatenate([stack(tz[c][:, :LANES]), stack(tz[c][:, LANES:])], axis=1),
         jnp.concatenate([zeros, v_s[c]], axis=1)], axis=0)) for c in cs]
    while links_done[0] < chunks or pending:
        advance()
    mt =[jnp.where(same_head, _mm_tn(tz[c][:, :LANES], bl_c[c]), 0.0) for c in cs]
    ct = [_mm_tn(jnp.concatenate([tz[c][:, LANES:].astype(BF16), v_c[c]], axis=0),
                 jnp.concatenate([bl_c[c], kl_c[c]], axis=0)) for c in cs]

    for c in cs:
        rp_st[slot_w, c] = (rt[c] + q[c][:, :LANES]).astype(BF16)
        yp_st[slot_w, c] = q[c][:, LANES:]
        mt_st[slot_w, c] = mt[c].astype(BF16)
        ct_st[slot_w, c] = jnp.where(head0, ct[c][:cl], ct[c][cl:])
        pl_st[slot_w, c] = jnp.broadcast_to(last[c], (8, LANES))


def _rwscan(r, lw, k, v, a, b, g, vecs, batch, seq, lt=256, pairs=4):
    t, d = r.shape
    nt = seq // lt
    width = pairs * LANES
    items = pairs * (lt // RW_CHUNK)
    pair = 2 * RW_CHUNK
    blk_in = pl.BlockSpec((lt, width), lambda bi, hp, ti: (bi * nt + jnp.minimum(ti, nt - 1), hp))
    blk_out = pl.BlockSpec((lt, width), lambda bi, hp, ti: (bi * nt + jnp.maximum(ti - 1, 0), hp))
    return pl.pallas_call(
        functools.partial(_rwscan_kernel, chunks=lt // RW_CHUNK),
        grid=(batch, d // width, nt + 1),
        in_specs=[blk_in] * 7 + [pl.BlockSpec((8, width), lambda bi, hp, ti: (0, hp))],
        out_specs=blk_out,
        out_shape=jax.ShapeDtypeStruct((t, d), BF16),
        scratch_shapes=[
            pltpu.VMEM((pairs, RW_HEAD, LANES), F32),
            pltpu.VMEM((2, items, RW_CHUNK, LANES), BF16),
            pltpu.VMEM((2, items, RW_CHUNK, LANES), F32),
            pltpu.VMEM((2, items, LANES, LANES), BF16),
            pltpu.VMEM((2, items, RW_HEAD, LANES), F32),
            pltpu.VMEM((2, items, 8, LANES), F32),
            pltpu.VMEM((2, lt, width), F32),
            pltpu.VMEM((2, lt, width), F32),
        ],
        compiler_params=_params("parallel", "parallel", "arbitrary"),
        name="rwscan",
    )(r, lw, k, v, a, b, g, vecs)


def _rope_kernel(pos_ref, inv_ref, cos_ref, sin_ref):
    ang = pos_ref[...].astype(F32) * inv_ref[...]
    cos_ref[...] = jnp.cos(ang)
    sin_ref[...] = jnp.sin(ang)


def _rope_tables(pos, inv_freq, tm=512):
    t = pos.shape[0]
    half = inv_freq.shape[1]
    out = jax.ShapeDtypeStruct((t, half), F32)
    return pl.pallas_call(
        _rope_kernel,
        grid=(t // tm,),
        in_specs=[pl.BlockSpec((tm, 1), lambda i: (i, 0)), pl.BlockSpec((1, half), lambda i: (0, 0))],
        out_specs=[pl.BlockSpec((tm, half), lambda i: (i, 0))] * 2,
        out_shape=[out, out],
        compiler_params=_params("parallel"),
        name="rope",
    )(pos, inv_freq)


RET_HEADS_PER_STEP = 2


def _retscan_kernel(q_ref, k_ref, v_ref, gate_ref, cos_ref, sin_ref, lg_ref, gn_ref, o_ref, st_ref, intra_ref):
    lc = q_ref.shape[0]
    heads = lg_ref.shape[0]
    dk = q_ref.shape[1] // heads
    dv = v_ref.shape[1] // heads
    half = dk // 2
    ic = lax.broadcasted_iota(jnp.int32, (lc, 1), 0).astype(F32)
    lgs = [lg_ref[h][:, 0:1] for h in range(heads)]

    @pl.when(pl.program_id(2) == 0)
    def _():
        st_ref[...] = jnp.zeros_like(st_ref)
        diff = ic - lax.broadcasted_iota(jnp.int32, (1, lc), 1).astype(F32)
        causal = diff >= 0
        for h in range(heads):
            intra_ref[h] = jnp.where(causal, jnp.exp(lgs[h] * jnp.where(causal, diff, 0.0)), 0.0)

    cos, sin = cos_ref[...], sin_ref[...]

    def rot(t):
        t1, t2 = t[:, :half], t[:, half:]
        return jnp.concatenate([t1 * cos - t2 * sin, t1 * sin + t2 * cos], axis=1)

    for h in range(heads):
        lg = lgs[h]
        q = rot(q_ref[:, h * dk:(h + 1) * dk].astype(F32))
        k = rot(k_ref[:, h * dk:(h + 1) * dk].astype(F32)) * (dk ** -0.5)
        v = v_ref[:, h * dv:(h + 1) * dv]
        xi = jnp.exp(lg * (ic + 1.0))
        zeta = jnp.exp(lg * (lc - 1.0 - ic))
        st = st_ref[h]
        sc = _mm_nt(q, k) * intra_ref[h]
        y = _mm(sc, v) + _mm(q, st) * xi
        st_ref[h] = jnp.exp(lg * lc) * st + _mm_tn(k * zeta, v)
        mean = jnp.mean(y, axis=-1, keepdims=True)
        yc = y - mean
        yn = yc * lax.rsqrt(jnp.mean(yc * yc, axis=-1, keepdims=True) + NORM_EPS) * gn_ref[:, h * dv:(h + 1) * dv]
        gate = gate_ref[:, h * dv:(h + 1) * dv].astype(F32)
        o_ref[:, h * dv:(h + 1) * dv] = (gate * _sigmoid(gate) * yn).astype(BF16)


def _retscan(proj, cos, sin, lg, gn_w, batch, seq, d, lc=256):
    t = proj.shape[0]
    nc = seq // lc
    hps = RET_HEADS_PER_STEP
    nhb = RET_HEADS // hps
    dk, dv = d // RET_HEADS, 2 * d // RET_HEADS
    tok = lambda bi, hb, ci: bi * nc + ci
    return pl.pallas_call(
        _retscan_kernel,
        grid=(batch, nhb, nc),
        in_specs=[
            pl.BlockSpec((lc, hps * dk), lambda bi, hb, ci: (tok(bi, hb, ci), hb)),
            pl.BlockSpec((lc, hps * dk), lambda bi, hb, ci: (tok(bi, hb, ci), nhb + hb)),
            pl.BlockSpec((lc, hps * dv), lambda bi, hb, ci: (tok(bi, hb, ci), nhb + hb)),
            pl.BlockSpec((lc, hps * dv), lambda bi, hb, ci: (tok(bi, hb, ci), 2 * nhb + hb)),
            pl.BlockSpec((lc, dk // 2), lambda bi, hb, ci: (tok(bi, hb, ci), 0)),
            pl.BlockSpec((lc, dk // 2), lambda bi, hb, ci: (tok(bi, hb, ci), 0)),
            pl.BlockSpec((hps, 1, LANES), lambda bi, hb, ci: (hb, 0, 0)),
            pl.BlockSpec((1, hps * dv), lambda bi, hb, ci: (0, hb)),
        ],
        out_specs=pl.BlockSpec((lc, hps * dv), lambda bi, hb, ci: (tok(bi, hb, ci), hb)),
        out_shape=jax.ShapeDtypeStruct((t, 2 * d), BF16),
        scratch_shapes=[pltpu.VMEM((hps, dk, dv), F32), pltpu.VMEM((hps, lc, lc), F32)],
        compiler_params=_params("parallel", "parallel", "arbitrary"),
        name="retscan",
    )(proj, proj, proj, proj, cos, sin, lg, gn_w)


def _mlscan_kernel(q_ref, k_ref, v_ref, og_ref, gt_ref, bif_ref, hn_ref, o_ref, c_ref, m_ref):
    lc = q_ref.shape[0]
    dqk = q_ref.shape[1] // ML_HEADS
    dv = v_ref.shape[1] // ML_HEADS

    @pl.when(pl.program_id(1) == 0)
    def _():
        c_ref[...] = jnp.zeros_like(c_ref)
        m_ref[...] = jnp.zeros_like(m_ref)

    gt = gt_ref[...] + bif_ref[...]
    gt = ML_GATE_CAP * jnp.tanh(gt * (1.0 / ML_GATE_CAP))
    lane = lax.broadcasted_iota(jnp.int32, gt.shape, 1)
    is_f = (lane >= ML_HEADS) & (lane < 2 * ML_HEADS)
    log_f = jnp.where(is_f, -_softplus(-gt), 0.0)
    bcum = pltpu.roll(_cumsum_rows(log_f), LANES - ML_HEADS, 1)
    rj_t = (bcum - gt).T
    ir = lax.broadcasted_iota(jnp.int32, (lc, lc), 0)
    ic = lax.broadcasted_iota(jnp.int32, (lc, lc), 1)
    causal = ir >= ic
    ones_col = jnp.where(lax.broadcasted_iota(jnp.int32, (lc, LANES), 1) == 0, 1.0, 0.0).astype(BF16)
    for h in range(ML_HEADS):
        q = q_ref[:, h * dqk:(h + 1) * dqk]
        k = k_ref[:, h * dqk:(h + 1) * dqk].astype(F32) * (dqk ** -0.5)
        vext = jnp.concatenate([v_ref[:, h * dv:(h + 1) * dv], ones_col], axis=1)
        b_col = bcum[:, h:h + 1]
        i_col = gt[:, h:h + 1]
        m_st = m_ref[h:h + 1, 0:1]
        log_d = jnp.where(causal, b_col - rj_t[h:h + 1, :], -jnp.inf)
        log_inter = b_col + m_st
        m_t = jnp.maximum(log_inter, jnp.max(log_d, axis=-1, keepdims=True))
        dmat = jnp.exp(log_d - m_t)
        w_inter = jnp.exp(log_inter - m_t)
        sc = _mm_nt(q, k) * dmat
        c_st = c_ref[h]
        num = _mm(sc, vext) + w_inter * _mm(q, c_st)
        dot = num[:, dv:dv + 1]
        hc = num[:, :dv] / jnp.maximum(jnp.abs(dot), jnp.exp(-m_t))
        m_new = m_t[lc - 1:lc, :]
        b_last = b_col[lc - 1:lc, :]
        w_s = jnp.exp(b_last - b_col + i_col - m_new)
        dec = jnp.exp(b_last + m_st - m_new)
        c_ref[h] = dec * c_st + _mm_tn(k * w_s, vext)
        m_ref[h:h + 1, :] = jnp.broadcast_to(m_new, (1, LANES))
        yf = hc * lax.rsqrt(jnp.mean(hc * hc, axis=-1, keepdims=True) + NORM_EPS)
        og = og_ref[:, h * dv:(h + 1) * dv].astype(F32)
        o_ref[:, h * dv:(h + 1) * dv] = (_sigmoid(og) * (yf * hn_ref[:, h * dv:(h + 1) * dv])).astype(BF16)


def _mlscan(proj, gates, bif, hn_w, batch, seq, d, lc=256):
    t = proj.shape[0]
    nc = seq // lc
    dq = d // 2
    tok = lambda bi, ci: bi * nc + ci
    return pl.pallas_call(
        _mlscan_kernel,
        grid=(batch, nc),
        in_specs=[
            pl.BlockSpec((lc, dq), lambda bi, ci: (tok(bi, ci), 0)),
            pl.BlockSpec((lc, dq), lambda bi, ci: (tok(bi, ci), 1)),
            pl.BlockSpec((lc, d), lambda bi, ci: (tok(bi, ci), 1)),
            pl.BlockSpec((lc, d), lambda bi, ci: (tok(bi, ci), 2)),
            pl.BlockSpec((lc, LANES), lambda bi, ci: (tok(bi, ci), 0)),
            pl.BlockSpec((1, LANES), lambda bi, ci: (0, 0)),
            pl.BlockSpec((1, d), lambda bi, ci: (0, 0)),
        ],
        out_specs=pl.BlockSpec((lc, d), lambda bi, ci: (tok(bi, ci), 0)),
        out_shape=jax.ShapeDtypeStruct((t, d), BF16),
        scratch_shapes=[pltpu.VMEM((ML_HEADS, dq // ML_HEADS, d // ML_HEADS + LANES), F32),
                        pltpu.VMEM((8, LANES), F32)],
        compiler_params=_params("parallel", "arbitrary"),
        name="mlscan",
    )(proj, proj, proj, proj, gates, bif, hn_w)


def _pad_cols(w, n):
    return jnp.pad(w, ((0, 0), (0, n - w.shape[1])))


def _pad_rows(w, n):
    return jnp.pad(w, ((0, n - w.shape[0]), (0, 0)))


def kernel(x, positions, norm_mix, norm_ffn, norm_final, rw_mu, rw_w_rkv, rw_w0, rw_w1, rw_w2, rw_a0, rw_a1, rw_a2, rw_g1, rw_g2, rw_k_k, rw_k_a, rw_r_k, rw_ln_w, rw_ln_b, rw_w_o, rw_v0, rw_v1, rw_v2, ret_w_in, ret_gn_w, ret_w_out, ml_w_in, ml_b_if, ml_hn_w, ml_w_out, ffn_w1, ffn_w3, ffn_w2):
    batch, seq, d = x.shape
    t = batch * seq
    depth = norm_mix.shape[0]
    xf = x.reshape(t, d)
    row = lambda vec: vec.reshape(1, -1).astype(F32)

    dk = d // RET_HEADS
    inv_freq = (1.0 / (ROPE_BASE ** jnp.linspace(0.0, 1.0, dk // 2, dtype=F32))).reshape(1, -1)
    cos, sin = _rope_tables(positions.reshape(t, 1), inv_freq)
    log_gamma = jnp.log1p(-jnp.exp2(-5.0 - jnp.arange(RET_HEADS, dtype=F32)))
    lg = jnp.broadcast_to(log_gamma[:, None, None], (RET_HEADS, 1, LANES))

    ffn_w = (ffn_w1[0].astype(BF16), ffn_w3[0].astype(BF16), ffn_w2[0].astype(BF16))
    rw_w_rkv_b = rw_w_rkv.astype(BF16)

    v_first = None
    for i in range(depth):
        kind, j = i % 3, i // 3
        g_mix = row(norm_mix[i])
        if kind == 0:
            has_vres = j > 0
            zeros_v1 = jnp.zeros((d, RW_LORA_G - RW_LORA_V), F32)
            zeros_v2 = jnp.zeros((RW_LORA_G - RW_LORA_V, d), F32)
            lin = jnp.concatenate([
                _pad_cols(rw_w1[j], RW_LORA_A - RW_LORA_W), _pad_cols(rw_a1[j], RW_LORA_V - RW_LORA_A),
                _pad_cols(rw_v1[j - 1], RW_LORA_G - RW_LORA_V) if has_vres else zeros_v1,
                rw_g1[j]], axis=1).astype(BF16)
            lout = jnp.concatenate([
                _pad_rows(rw_w2[j], RW_LORA_A - RW_LORA_W), _pad_rows(rw_a2[j], RW_LORA_V - RW_LORA_A),
                _pad_rows(rw_v2[j - 1], RW_LORA_G - RW_LORA_V) if has_vres else zeros_v2,
                rw_g2[j]], axis=0).astype(BF16)
            v0 = rw_v0[j - 1] if has_vres else jnp.zeros((d,), F32)
            zero = jnp.zeros((d,), F32)
            vecs = jnp.stack([rw_w0[j], rw_a0[j], v0, rw_k_k[j], rw_k_a[j], zero, zero, zero]).astype(F32)
            r, lw, k, v, a, b, g = _rwproj(xf, g_mix, rw_mu[j].astype(F32), rw_w_rkv_b, j, lin, lout,
                                           vecs, v_first if has_vres else None, seq)
            if j == 0:
                v_first = v
            svecs = jnp.stack([rw_r_k[j].reshape(-1), rw_ln_w[j], rw_ln_b[j], zero, zero, zero, zero, zero]).astype(F32)
            z = _rwscan(r, lw, k, v, a, b, g, svecs, batch, seq)
            xf = _outproj(z, rw_w_o[j].astype(BF16), xf)
        elif kind == 1:
            (proj,) = _normproj(xf, g_mix, ret_w_in[j].astype(BF16), tn=1024)
            z = _retscan(proj, cos, sin, lg, row(ret_gn_w[j]), batch, seq, d)
            xf = _outproj(z, ret_w_out[j].astype(BF16), xf)
        else:
            n_pad = 3 * d + 2 * LANES
            tn = n_pad // 5
            w_in = _pad_cols(ml_w_in[j], n_pad).astype(BF16)
            gate_lo = 3 * d - (n_pad - tn)
            proj, gates = _normproj(xf, g_mix, w_in, tn=tn, side_cols=(gate_lo, gate_lo + LANES))
            bif = _pad_cols(ml_b_if[j].reshape(1, -1).astype(F32), LANES)
            z = _mlscan(proj, gates, bif, row(ml_hn_w[j]), batch, seq, d)
            xf = _outproj(z, ml_w_out[j].astype(BF16), xf)
        nxt = (ffn_w1, ffn_w3, ffn_w2, i + 1) if i + 1 < depth else None
        xf, *ffn_w = _ffn(xf, row(norm_ffn[i]), *ffn_w, row(norm_final), final_norm=(i == depth - 1),
                          next_weights=nxt)
    return xf.reshape(batch, seq, d)
```

```python
import functools

import jax
import jax.numpy as jnp
from jax import lax
from jax.experimental import pallas as pl
from jax.experimental.pallas import tpu as pltpu

F32 = jnp.float32
BF16 = jnp.bfloat16

NORM_EPS = 1e-6
RW_HEAD = 64
RW_GN_EPS = 64e-5
RW_CHUNK = 64
RW_INV_BASE = 8
RET_HEADS = 8
ROPE_BASE = 10000.0
ML_HEADS = 4
ML_GATE_CAP = 15.0
LANES = 128
NORM_ROWS = 256
VMEM_LIMIT = 56 * 1024 * 1024


def _params(*sem):
    return pltpu.CompilerParams(dimension_semantics=sem, vmem_limit_bytes=VMEM_LIMIT)


def _mm(a, b):
    return jnp.dot(a.astype(BF16), b.astype(BF16), preferred_element_type=F32)


def _mm_nt(a, b):
    return lax.dot_general(a.astype(BF16), b.astype(BF16), (((1,), (1,)), ((), ())),
                           preferred_element_type=F32)


def _mm_tn(a, b):
    return lax.dot_general(a.astype(BF16), b.astype(BF16), (((0,), (0,)), ((), ())),
                           preferred_element_type=F32)


def _rms(x, g):
    return x * lax.rsqrt(jnp.mean(x * x, axis=-1, keepdims=True) + NORM_EPS) * g


def _sigmoid(x):
    return 0.5 * jnp.tanh(0.5 * x) + 0.5


def _softplus(x):
    return jnp.maximum(x, 0.0) + jnp.log(1.0 + jnp.exp(-jnp.abs(x)))


def _cumsum_rows(x):
    n = x.shape[0]
    row = lax.broadcasted_iota(jnp.int32, x.shape, 0)
    s = 1
    while s < n:
        x = x + jnp.where(row >= s, pltpu.roll(x, s, 0), 0.0)
        s *= 2
    return x


def _seg64_sum(x, split=True):
    r = lax.broadcasted_iota(jnp.int32, (LANES, LANES), 0) // RW_HEAD
    c = lax.broadcasted_iota(jnp.int32, (LANES, LANES), 1) // RW_HEAD
    bd = jnp.where(r == c, 1.0, 0.0).astype(BF16)
    hi = x.astype(BF16)
    out = jnp.dot(hi, bd, preferred_element_type=F32)
    if split:
        out = out + jnp.dot((x - hi.astype(F32)).astype(BF16), bd, preferred_element_type=F32)
    return out


def _ffn_kernel(*refs, final_norm, cast_next):
    if cast_next:
        (x_ref, g_ref, w1_ref, w3_ref, w2_ref, gf_ref, n1_ref, n3_ref, n2_ref,
         o_ref, n1_o, n3_o, n2_o, h_s) = refs
        n1_o[...] = n1_ref[...].astype(BF16)
        n3_o[...] = n3_ref[...].astype(BF16)
        n2_o[...] = n2_ref[...].astype(BF16)
    else:
        x_ref, g_ref, w1_ref, w3_ref, w2_ref, gf_ref, o_ref, h_s = refs
    j = pl.program_id(1)
    tm = x_ref.shape[0]
    half = w1_ref.shape[1] // 2

    @pl.when(j == 0)
    def _():
        for s in range(0, tm, NORM_ROWS):
            x = x_ref[s:s + NORM_ROWS, :]
            h_s[s:s + NORM_ROWS, :] = _rms(x, g_ref[...]).astype(BF16)
            o_ref[s:s + NORM_ROWS, :] = x

    h = h_s[...]
    gate_up = [(jnp.dot(h, w1_ref[:, c:c + half], preferred_element_type=F32),
                jnp.dot(h, w3_ref[:, c:c + half], preferred_element_type=F32)) for c in (0, half)]
    down = None
    for n, (a, b) in enumerate(gate_up):
        act = (a * _sigmoid(a) * b).astype(BF16)
        part = jnp.dot(act, w2_ref[n * half:(n + 1) * half, :], preferred_element_type=F32)
        down = part if down is None else down + part
    o_ref[...] += down

    if final_norm:
        @pl.when(j == pl.num_programs(1) - 1)
        def _():
            for s in range(0, tm, NORM_ROWS):
                o_ref[s:s + NORM_ROWS, :] = _rms(o_ref[s:s + NORM_ROWS, :], gf_ref[...])


def _ffn(x, g, w1, w3, w2, gf, final_norm, next_weights=None, tm=1024, tf=512):
    t, d = x.shape
    ff = w1.shape[1]
    tf = min(tf, ff)
    ni = t // tm
    in_specs = [
        pl.BlockSpec((tm, d), lambda i, j: (i, 0), pipeline_mode=pl.Buffered(1)),
        pl.BlockSpec((1, d), lambda i, j: (0, 0)),
        pl.BlockSpec((d, tf), lambda i, j: (0, j)),
        pl.BlockSpec((d, tf), lambda i, j: (0, j)),
        pl.BlockSpec((tf, d), lambda i, j: (j, 0)),
        pl.BlockSpec((1, d), lambda i, j: (0, 0)),
    ]
    out_specs = [pl.BlockSpec((tm, d), lambda i, j: (i, 0))]
    out_shape = [jax.ShapeDtypeStruct((t, d), F32)]
    args = [x, g, w1, w3, w2, gf]
    if next_weights is not None:
        w1s, w3s, w2s, layer = next_weights
        rb = d // ni
        assert rb * ni == d and rb % LANES == 0
        in_specs += [pl.BlockSpec((None, rb, tf), lambda i, j: (layer, i, j)),
                     pl.BlockSpec((None, rb, tf), lambda i, j: (layer, i, j)),
                     pl.BlockSpec((None, tf, rb), lambda i, j: (layer, j, i))]
        out_specs += [pl.BlockSpec((rb, tf), lambda i, j: (i, j)),
                      pl.BlockSpec((rb, tf), lambda i, j: (i, j)),
                      pl.BlockSpec((tf, rb), lambda i, j: (j, i))]
        out_shape += [jax.ShapeDtypeStruct((d, ff), BF16), jax.ShapeDtypeStruct((d, ff), BF16),
                      jax.ShapeDtypeStruct((ff, d), BF16)]
        args += [w1s, w3s, w2s]
    return pl.pallas_call(
        functools.partial(_ffn_kernel, final_norm=final_norm, cast_next=next_weights is not None),
        grid=(ni, ff // tf),
        in_specs=in_specs,
        out_specs=out_specs,
        out_shape=out_shape,
        scratch_shapes=[pltpu.VMEM((tm, d), BF16)],
        compiler_params=_params("parallel", "arbitrary"),
        name="ffn",
    )(*args)


def _outproj_kernel(z_ref, w_ref, x_ref, o_ref):
    o_ref[...] = x_ref[...] + jnp.dot(z_ref[...], w_ref[...], preferred_element_type=F32)


OUTPROJ_W_BYTES = 8 * 1024 * 1024


def _outproj(z, w, x, tm=512):
    t, k = z.shape
    d = w.shape[1]
    tn = min(d, OUTPROJ_W_BYTES // (2 * k))
    return pl.pallas_call(
        _outproj_kernel,
        grid=(d // tn, t // tm),
        in_specs=[
            pl.BlockSpec((tm, k), lambda j, i: (i, 0)),
            pl.BlockSpec((k, tn), lambda j, i: (0, j)),
            pl.BlockSpec((tm, tn), lambda j, i: (i, j)),
        ],
        out_specs=pl.BlockSpec((tm, tn), lambda j, i: (i, j)),
        out_shape=jax.ShapeDtypeStruct((t, d), F32),
        compiler_params=_params("arbitrary", "arbitrary"),
        name="outproj",
    )(z, w, x)


def _normproj_kernel(x_ref, g_ref, w_ref, o_ref, *rest, side_cols):
    h_s = rest[-1]

    @pl.when(pl.program_id(1) == 0)
    def _():
        for s in range(0, x_ref.shape[0], NORM_ROWS):
            h_s[s:s + NORM_ROWS, :] = _rms(x_ref[s:s + NORM_ROWS, :], g_ref[...]).astype(BF16)

    acc = jnp.dot(h_s[...], w_ref[...], preferred_element_type=F32)
    o_ref[...] = acc.astype(o_ref.dtype)
    if side_cols:
        @pl.when(pl.program_id(1) == pl.num_programs(1) - 1)
        def _():
            rest[0][...] = acc[:, side_cols[0]:side_cols[1]]


def _normproj(x, g, w, tn, side_cols=None, tm=1024):
    t, d = x.shape
    n = w.shape[1]
    out_specs = [pl.BlockSpec((tm, tn), lambda i, j: (i, j))]
    out_shape = [jax.ShapeDtypeStruct((t, n), BF16)]
    if side_cols:
        width = side_cols[1] - side_cols[0]
        out_specs.append(pl.BlockSpec((tm, width), lambda i, j: (i, 0)))
        out_shape.append(jax.ShapeDtypeStruct((t, width), F32))
    return pl.pallas_call(
        functools.partial(_normproj_kernel, side_cols=side_cols),
        grid=(t // tm, n // tn),
        in_specs=[
            pl.BlockSpec((tm, d), lambda i, j: (i, 0)),
            pl.BlockSpec((1, d), lambda i, j: (0, 0)),
            pl.BlockSpec((d, tn), lambda i, j: (0, j)),
        ],
        out_specs=out_specs,
        out_shape=out_shape,
        scratch_shapes=[pltpu.VMEM((tm, d), BF16)],
        compiler_params=_params("parallel", "arbitrary"),
        name="normproj",
    )(x, g, w)


RW_LORA_W = 0
RW_LORA_A = 128
RW_LORA_V = 256
RW_LORA_G = 384
RW_LORA_END = 640
RW_PROLOGUE_ROWS = 128


def _rwproj_kernel(*refs, tiles_per_seq, has_vres):
    if has_vres:
        (x_ref, xp_ref, g_ref, mu_ref, wrkv_ref, lin_ref, lout_ref, vec_ref, vf_ref,
         r_o, lw_o, k_o, v_o, a_o, b_o, g_o, xr_s, xk_s, xv_s, l1_s) = refs
    else:
        (x_ref, xp_ref, g_ref, mu_ref, wrkv_ref, lin_ref, lout_ref, vec_ref,
         r_o, lw_o, k_o, v_o, a_o, b_o, g_o, xr_s, xk_s, xv_s, l1_s) = refs
        vf_ref = None
    i = pl.program_id(0)

    @pl.when(pl.program_id(1) == 0)
    def _():
        g = g_ref[...]
        mu = mu_ref[...]
        sub = RW_PROLOGUE_ROWS
        for s in range(x_ref.shape[0] // sub):
            rows = slice(s * sub, (s + 1) * sub)
            h = _rms(x_ref[rows, :], g)
            if s == 0:
                hp = _rms(xp_ref[7:8, :], g)
                hp = jnp.where(i % tiles_per_seq == 0, 0.0, hp)
            else:
                hp = _rms(x_ref[s * sub - 8:s * sub, :], g)[7:8, :]
            row = lax.broadcasted_iota(jnp.int32, h.shape, 0)
            xx = jnp.where(row == 0, hp, pltpu.roll(h, 1, 0)) - h
            xr_s[rows, :] = (h + xx * mu[0:1]).astype(BF16)
            xk_s[rows, :] = (h + xx * mu[2:3]).astype(BF16)
            xv = (h + xx * mu[3:4]).astype(BF16)
            xv_s[rows, :] = xv
            xw = h + xx * mu[1:2]
            xa = h + xx * mu[4:5]
            xg = h + xx * mu[5:6]
            l1_s[rows, RW_LORA_W:RW_LORA_A] = jnp.tanh(_mm(xw, lin_ref[:, RW_LORA_W:RW_LORA_A])).astype(BF16)
            l1_s[rows, RW_LORA_A:RW_LORA_V] = _mm(xa, lin_ref[:, RW_LORA_A:RW_LORA_V]).astype(BF16)
            l1_s[rows, RW_LORA_V:RW_LORA_G] = _mm(xv, lin_ref[:, RW_LORA_V:RW_LORA_G]).astype(BF16)
            l1_s[rows, RW_LORA_G:RW_LORA_END] = _sigmoid(
                _mm(xg, lin_ref[:, RW_LORA_G:RW_LORA_END])).astype(BF16)

    r = jnp.dot(xr_s[...], wrkv_ref[0], preferred_element_type=F32)
    k = jnp.dot(xk_s[...], wrkv_ref[1], preferred_element_type=F32)
    v = jnp.dot(xv_s[...], wrkv_ref[2], preferred_element_type=F32)
    vec = vec_ref[...]
    w0, a0, v0, k_k, k_a = (vec[n:n + 1] for n in range(5))

    def lora2(lo, hi):
        return jnp.dot(l1_s[:, lo:hi], lout_ref[lo:hi, :], preferred_element_type=F32)

    w_log = -_softplus(-(w0 + lora2(RW_LORA_W, RW_LORA_A))) - 0.5
    lw_o[...] = -jnp.exp(w_log)
    a = _sigmoid(a0 + lora2(RW_LORA_A, RW_LORA_V))
    if has_vres:
        v = v + (vf_ref[...].astype(F32) - v) * _sigmoid(v0 + lora2(RW_LORA_V, RW_LORA_G))
    g_o[...] = lora2(RW_LORA_G, RW_LORA_END).astype(g_o.dtype)
    kk = k * k_k
    tn = kk.shape[1]
    ss = jnp.concatenate([_seg64_sum(jnp.square(kk[:, c:c + LANES]), split=False) for c in range(0, tn, LANES)],
                         axis=1)
    kk = kk * lax.rsqrt(jnp.maximum(ss, 1e-24))
    r_o[...] = r.astype(r_o.dtype)
    k_o[...] = (k * (1.0 + (a - 1.0) * k_a)).astype(k_o.dtype)
    v_o[...] = v.astype(v_o.dtype)
    a_o[...] = (-kk).astype(a_o.dtype)
    b_o[...] = (kk * a).astype(b_o.dtype)


def _rwproj(x, g, mu, wrkv, layer, lin, lout, vecs, vfirst, seq, tm=512, tn=512):
    t, d = x.shape
    has_vres = vfirst is not None
    rows8 = tm // 8
    in_specs = [
        pl.BlockSpec((tm, d), lambda i, j: (i, 0)),
        pl.BlockSpec((8, d), lambda i, j: (jnp.maximum(i * rows8 - 1, 0), 0)),
        pl.BlockSpec((1, d), lambda i, j: (0, 0)),
        pl.BlockSpec((6, d), lambda i, j: (0, 0)),
        pl.BlockSpec((None, 3, d, tn), lambda i, j: (layer, 0, 0, j)),
        pl.BlockSpec((d, RW_LORA_END), lambda i, j: (0, 0)),
        pl.BlockSpec((RW_LORA_END, tn), lambda i, j: (0, j)),
        pl.BlockSpec((8, tn), lambda i, j: (0, j)),
    ]
    args = [x, x, g, mu, wrkv, lin, lout, vecs]
    if has_vres:
        in_specs.append(pl.BlockSpec((tm, tn), lambda i, j: (i, j)))
        args.append(vfirst)
    out_dtypes = [BF16, F32, BF16, BF16, BF16, BF16, BF16]
    return pl.pallas_call(
        functools.partial(_rwproj_kernel, tiles_per_seq=seq // tm, has_vres=has_vres),
        grid=(t // tm, d // tn),
        in_specs=in_specs,
        out_specs=[pl.BlockSpec((tm, tn), lambda i, j: (i, j))] * 7,
        out_shape=[jax.ShapeDtypeStruct((t, d), dt) for dt in out_dtypes],
        scratch_shapes=[pltpu.VMEM((tm, d), BF16)] * 3 + [pltpu.VMEM((tm, RW_LORA_END), BF16)],
        compiler_params=_params("parallel", "arbitrary"),
        name="rwproj",
    )(*args)


def _rwscan_kernel(r_ref, lw_ref, k_ref, v_ref, a_ref, b_ref, g_ref, vec_ref, o_ref,
                   s_ref, rp_st, yp_st, mt_st, ct_st, pl_st, bg_st, g_st, *, chunks):
    cl = RW_CHUNK
    pair = 2 * cl
    lt = chunks * cl
    width = lw_ref.shape[1]
    pairs = width // LANES
    ti = pl.program_id(2)
    slot_w = ti % 2
    slot_r = 1 - slot_w

    @pl.when(ti == 0)
    def _():
        s_ref[...] = jnp.zeros_like(s_ref)
        for ref in (rp_st, yp_st, mt_st, ct_st, pl_st, bg_st, g_st):
            ref[1] = jnp.zeros(ref.shape[1:], ref.dtype)

    head0 = lax.broadcasted_iota(jnp.int32, (cl, LANES), 1) < RW_HEAD

    def stack(x):
        return jnp.concatenate([jnp.where(head0, x, 0.0), jnp.where(head0, 0.0, x)], axis=0).astype(BF16)

    vec = vec_ref[...]
    states = [s_ref[p] for p in range(pairs)]
    ys = [[] for _ in range(pairs)]
    links_done = [0]

    def serial_link():
        j = links_done[0]
        if j >= chunks:
            return
        links_done[0] = j + 1
        for p in range(pairs):
            c = p * chunks + j
            st = states[p]
            ys[p].append(_mm_nt(rp_st[slot_r, c], stack(st)) + yp_st[slot_r, c])
            states[p] = jnp.exp(pl_st[slot_r, c][0:1]) * st + _mm(st, mt_st[slot_r, c]) + ct_st[slot_r, c]

    norm = {}

    def finish_mean():
        while links_done[0] < chunks:
            serial_link()
        for p in range(pairs):
            s_ref[p] = states[p]
        norm["y"] = [jnp.concatenate(ys[p], axis=0) for p in range(pairs)]
        norm["mean"] = [_seg64_sum(y) * (1.0 / RW_HEAD) for y in norm["y"]]

    def finish_var():
        norm["yc"] = [norm["y"][p] - norm["mean"][p] for p in range(pairs)]
        norm["var"] = [_seg64_sum(yc * yc, split=False) * (1.0 / RW_HEAD) for yc in norm["yc"]]

    def finish_store():
        for p in range(pairs):
            lanes = slice(p * LANES, (p + 1) * LANES)
            yn = norm["yc"][p] * lax.rsqrt(norm["var"][p] + RW_GN_EPS) * vec[1:2, lanes] + vec[2:3, lanes]
            o_ref[:, lanes] = (yn * g_st[slot_r, :, lanes] + bg_st[slot_r, :, lanes]).astype(BF16)

    pending = [finish_mean, finish_var, finish_store]

    def advance():
        if links_done[0] < chunks:
            serial_link()
        elif pending:
            pending.pop(0)()

    advance()

    ri = lax.broadcasted_iota(jnp.int32, (cl, LANES), 0)
    ci = lax.broadcasted_iota(jnp.int32, (cl, LANES), 1) % cl
    strict = ri > ci
    incl = ri >= ci
    eye = jnp.where(ri == ci, 1.0, 0.0)
    rb = lax.broadcasted_iota(jnp.int32, (pair, LANES), 0) // cl
    cb = lax.broadcasted_iota(jnp.int32, (pair, LANES), 1) // RW_HEAD
    same_head = rb == cb

    cs = range(pairs * chunks)
    rows = [(slice((c % chunks) * cl, (c % chunks + 1) * cl),
             slice((c // chunks) * LANES, (c // chunks + 1) * LANES)) for c in cs]

    def prepare(c):
        tok, lanes = rows[c]
        lw = lw_ref[tok, lanes]
        cum = lw
        s = 1
        while s < cl:
            cum = cum + jnp.where(ri >= s, pltpu.roll(cum, s, 0), 0.0)
            s *= 2
        r, k, v = (ref[tok, lanes].astype(F32) for ref in (r_ref, k_ref, v_ref))
        a, b, g = (ref[tok, lanes].astype(F32) for ref in (a_ref, b_ref, g_ref))
        bonus = _seg64_sum(r * k * vec[0:1, lanes], split=False)
        bg_st[slot_w, tok, lanes] = bonus * v * g
        g_st[slot_w, tok, lanes] = g
        last = cum[cl - 1:cl, :]
        e_neg = jnp.exp(-cum)
        tail = jnp.exp(last - cum)
        at = a * jnp.exp(cum - lw)
        rt = r * jnp.exp(cum)
        gram = _mm_nt(jnp.concatenate([at, rt], axis=0),
                      jnp.concatenate([stack(b * e_neg), stack(k * e_neg)], axis=0))
        return dict(at=at, rt=rt, last=last, gram=gram, v_c=v.astype(BF16), v_s=stack(v),
                    bl_c=(b * tail).astype(BF16), kl_c=(k * tail).astype(BF16))

    items = []
    for c in cs:
        items.append(prepare(c))
        if c % 4 == 3:
            advance()
    at, rt, last, gram = ([it[name] for it in items] for name in ("at", "rt", "last", "gram"))
    v_c, v_s, bl_c, kl_c = ([it[name] for it in items] for name in ("v_c", "v_s", "bl_c", "kl_c"))
    n_ab = [jnp.where(strict, gram[c][:cl, :LANES], 0.0) for c in cs]
    a_ak = [jnp.where(strict, gram[c][:cl, LANES:], 0.0) for c in cs]
    a_r = [jnp.concatenate([jnp.where(incl, gram[c][cl:, :LANES], 0.0),
                            jnp.where(incl, gram[c][cl:, LANES:], 0.0)], axis=1).astype(BF16) for c in cs]
    x1 = [_mm(a_ak[c], v_s[c]) for c in cs]
    advance()
    blk_r, blk_c = ri // RW_INV_BASE, ci // RW_INV_BASE
    n_d = [jnp.where(blk_r == blk_c, n_ab[c], 0.0) for c in cs]
    tinv = [eye + n_d[c] for c in cs]
    pw = [_mm(n_d[c], stack(n_d[c])) for c in cs]
    advance()
    span = 4
    while span < RW_INV_BASE:
        both = [_mm(jnp.concatenate([pw[c], tinv[c]], axis=0), stack(pw[c])) for c in cs]
        pw = [both[c][:cl] for c in cs]
        tinv = [tinv[c] + both[c][cl:] for c in cs]
        span *= 2
        advance()
    tinv = [tinv[c] + _mm(tinv[c], stack(pw[c])) for c in cs]
    size = RW_INV_BASE
    while size < cl:
        lower_left = ((ri // (2 * size)) == (ci // (2 * size))) & ((ri // size) == (ci // size) + 1)
        advance()
        e = [_mm(jnp.where(lower_left, n_ab[c], 0.0), stack(tinv[c])) for c in cs]
        tinv = [tinv[c] + _mm(tinv[c], stack(e[c])) for c in cs]
        size *= 2
    advance()
    tz = [_mm(tinv[c], jnp.concatenate([stack(at[c]), stack(x1[c])], axis=1)) for c in cs]
    advance()
    zeros = jnp.zeros((pair, LANES), BF16)
    q = [_mm(a_r[c], jnp.concatenate(
        [jnp.concatenate([stack(tz[c][:, :LANES]), stack(tz[c][:, LANES:])], axis=1),
         jnp.concatenate([zeros, v_s[c]], axis=1)], axis=0)) for c in cs]
    while links_done[0] < chunks or pending:
        advance()
    mt =[jnp.where(same_head, _mm_tn(tz[c][:, :LANES], bl_c[c]), 0.0) for c in cs]
    ct = [_mm_tn(jnp.concatenate([tz[c][:, LANES:].astype(BF16), v_c[c]], axis=0),
                 jnp.concatenate([bl_c[c], kl_c[c]], axis=0)) for c in cs]

    for c in cs:
        rp_st[slot_w, c] = (rt[c] + q[c][:, :LANES]).astype(BF16)
        yp_st[slot_w, c] = q[c][:, LANES:]
        mt_st[slot_w, c] = mt[c].astype(BF16)
        ct_st[slot_w, c] = jnp.where(head0, ct[c][:cl], ct[c][cl:])
        pl_st[slot_w, c] = jnp.broadcast_to(last[c], (8, LANES))


def _rwscan(r, lw, k, v, a, b, g, vecs, batch, seq, lt=256, pairs=4):
    t, d = r.shape
    nt = seq // lt
    width = pairs * LANES
    items = pairs * (lt // RW_CHUNK)
    pair = 2 * RW_CHUNK
    blk_in = pl.BlockSpec((lt, width), lambda bi, hp, ti: (bi * nt + jnp.minimum(ti, nt - 1), hp))
    blk_out = pl.BlockSpec((lt, width), lambda bi, hp, ti: (bi * nt + jnp.maximum(ti - 1, 0), hp))
    return pl.pallas_call(
        functools.partial(_rwscan_kernel, chunks=lt // RW_CHUNK),
        grid=(batch, d // width, nt + 1),
        in_specs=[blk_in] * 7 + [pl.BlockSpec((8, width), lambda bi, hp, ti: (0, hp))],
        out_specs=blk_out,
        out_shape=jax.ShapeDtypeStruct((t, d), BF16),
        scratch_shapes=[
            pltpu.VMEM((pairs, RW_HEAD, LANES), F32),
            pltpu.VMEM((2, items, RW_CHUNK, LANES), BF16),
            pltpu.VMEM((2, items, RW_CHUNK, LANES), F32),
            pltpu.VMEM((2, items, LANES, LANES), BF16),
            pltpu.VMEM((2, items, RW_HEAD, LANES), F32),
            pltpu.VMEM((2, items, 8, LANES), F32),
            pltpu.VMEM((2, lt, width), F32),
            pltpu.VMEM((2, lt, width), F32),
        ],
        compiler_params=_params("parallel", "parallel", "arbitrary"),
        name="rwscan",
    )(r, lw, k, v, a, b, g, vecs)


def _rope_kernel(pos_ref, inv_ref, cos_ref, sin_ref):
    ang = pos_ref[...].astype(F32) * inv_ref[...]
    cos_ref[...] = jnp.cos(ang)
    sin_ref[...] = jnp.sin(ang)


def _rope_tables(pos, inv_freq, tm=512):
    t = pos.shape[0]
    half = inv_freq.shape[1]
    out = jax.ShapeDtypeStruct((t, half), F32)
    return pl.pallas_call(
        _rope_kernel,
        grid=(t // tm,),
        in_specs=[pl.BlockSpec((tm, 1), lambda i: (i, 0)), pl.BlockSpec((1, half), lambda i: (0, 0))],
        out_specs=[pl.BlockSpec((tm, half), lambda i: (i, 0))] * 2,
        out_shape=[out, out],
        compiler_params=_params("parallel"),
        name="rope",
    )(pos, inv_freq)


RET_HEADS_PER_STEP = 2


def _retscan_kernel(q_ref, k_ref, v_ref, gate_ref, cos_ref, sin_ref, lg_ref, gn_ref, o_ref, st_ref, intra_ref):
    lc = q_ref.shape[0]
    heads = lg_ref.shape[0]
    dk = q_ref.shape[1] // heads
    dv = v_ref.shape[1] // heads
    half = dk // 2
    ic = lax.broadcasted_iota(jnp.int32, (lc, 1), 0).astype(F32)
    lgs = [lg_ref[h][:, 0:1] for h in range(heads)]

    @pl.when(pl.program_id(2) == 0)
    def _():
        st_ref[...] = jnp.zeros_like(st_ref)
        diff = ic - lax.broadcasted_iota(jnp.int32, (1, lc), 1).astype(F32)
        causal = diff >= 0
        for h in range(heads):
            intra_ref[h] = jnp.where(causal, jnp.exp(lgs[h] * jnp.where(causal, diff, 0.0)), 0.0)

    cos, sin = cos_ref[...], sin_ref[...]

    def rot(t):
        t1, t2 = t[:, :half], t[:, half:]
        return jnp.concatenate([t1 * cos - t2 * sin, t1 * sin + t2 * cos], axis=1)

    for h in range(heads):
        lg = lgs[h]
        q = rot(q_ref[:, h * dk:(h + 1) * dk].astype(F32))
        k = rot(k_ref[:, h * dk:(h + 1) * dk].astype(F32)) * (dk ** -0.5)
        v = v_ref[:, h * dv:(h + 1) * dv]
        xi = jnp.exp(lg * (ic + 1.0))
        zeta = jnp.exp(lg * (lc - 1.0 - ic))
        st = st_ref[h]
        sc = _mm_nt(q, k) * intra_ref[h]
        y = _mm(sc, v) + _mm(q, st) * xi
        st_ref[h] = jnp.exp(lg * lc) * st + _mm_tn(k * zeta, v)
        mean = jnp.mean(y, axis=-1, keepdims=True)
        yc = y - mean
        yn = yc * lax.rsqrt(jnp.mean(yc * yc, axis=-1, keepdims=True) + NORM_EPS) * gn_ref[:, h * dv:(h + 1) * dv]
        gate = gate_ref[:, h * dv:(h + 1) * dv].astype(F32)
        o_ref[:, h * dv:(h + 1) * dv] = (gate * _sigmoid(gate) * yn).astype(BF16)


def _retscan(proj, cos, sin, lg, gn_w, batch, seq, d, lc=256):
    t = proj.shape[0]
    nc = seq // lc
    hps = RET_HEADS_PER_STEP
    nhb = RET_HEADS // hps
    dk, dv = d // RET_HEADS, 2 * d // RET_HEADS
    tok = lambda bi, hb, ci: bi * nc + ci
    return pl.pallas_call(
        _retscan_kernel,
        grid=(batch, nhb, nc),
        in_specs=[
            pl.BlockSpec((lc, hps * dk), lambda bi, hb, ci: (tok(bi, hb, ci), hb)),
            pl.BlockSpec((lc, hps * dk), lambda bi, hb, ci: (tok(bi, hb, ci), nhb + hb)),
            pl.BlockSpec((lc, hps * dv), lambda bi, hb, ci: (tok(bi, hb, ci), nhb + hb)),
            pl.BlockSpec((lc, hps * dv), lambda bi, hb, ci: (tok(bi, hb, ci), 2 * nhb + hb)),
            pl.BlockSpec((lc, dk // 2), lambda bi, hb, ci: (tok(bi, hb, ci), 0)),
            pl.BlockSpec((lc, dk // 2), lambda bi, hb, ci: (tok(bi, hb, ci), 0)),
            pl.BlockSpec((hps, 1, LANES), lambda bi, hb, ci: (hb, 0, 0)),
            pl.BlockSpec((1, hps * dv), lambda bi, hb, ci: (0, hb)),
        ],
        out_specs=pl.BlockSpec((lc, hps * dv), lambda bi, hb, ci: (tok(bi, hb, ci), hb)),
        out_shape=jax.ShapeDtypeStruct((t, 2 * d), BF16),
        scratch_shapes=[pltpu.VMEM((hps, dk, dv), F32), pltpu.VMEM((hps, lc, lc), F32)],
        compiler_params=_params("parallel", "parallel", "arbitrary"),
        name="retscan",
    )(proj, proj, proj, proj, cos, sin, lg, gn_w)


def _mlscan_kernel(q_ref, k_ref, v_ref, og_ref, gt_ref, bif_ref, hn_ref, o_ref, c_ref, m_ref):
    lc = q_ref.shape[0]
    dqk = q_ref.shape[1] // ML_HEADS
    dv = v_ref.shape[1] // ML_HEADS

    @pl.when(pl.program_id(1) == 0)
    def _():
        c_ref[...] = jnp.zeros_like(c_ref)
        m_ref[...] = jnp.zeros_like(m_ref)

    gt = gt_ref[...] + bif_ref[...]
    gt = ML_GATE_CAP * jnp.tanh(gt * (1.0 / ML_GATE_CAP))
    lane = lax.broadcasted_iota(jnp.int32, gt.shape, 1)
    is_f = (lane >= ML_HEADS) & (lane < 2 * ML_HEADS)
    log_f = jnp.where(is_f, -_softplus(-gt), 0.0)
    bcum = pltpu.roll(_cumsum_rows(log_f), LANES - ML_HEADS, 1)
    rj_t = (bcum - gt).T
    ir = lax.broadcasted_iota(jnp.int32, (lc, lc), 0)
    ic = lax.broadcasted_iota(jnp.int32, (lc, lc), 1)
    causal = ir >= ic
    ones_col = jnp.where(lax.broadcasted_iota(jnp.int32, (lc, LANES), 1) == 0, 1.0, 0.0).astype(BF16)
    for h in range(ML_HEADS):
        q = q_ref[:, h * dqk:(h + 1) * dqk]
        k = k_ref[:, h * dqk:(h + 1) * dqk].astype(F32) * (dqk ** -0.5)
        vext = jnp.concatenate([v_ref[:, h * dv:(h + 1) * dv], ones_col], axis=1)
        b_col = bcum[:, h:h + 1]
        i_col = gt[:, h:h + 1]
        m_st = m_ref[h:h + 1, 0:1]
        log_d = jnp.where(causal, b_col - rj_t[h:h + 1, :], -jnp.inf)
        log_inter = b_col + m_st
        m_t = jnp.maximum(log_inter, jnp.max(log_d, axis=-1, keepdims=True))
        dmat = jnp.exp(log_d - m_t)
        w_inter = jnp.exp(log_inter - m_t)
        sc = _mm_nt(q, k) * dmat
        c_st = c_ref[h]
        num = _mm(sc, vext) + w_inter * _mm(q, c_st)
        dot = num[:, dv:dv + 1]
        hc = num[:, :dv] / jnp.maximum(jnp.abs(dot), jnp.exp(-m_t))
        m_new = m_t[lc - 1:lc, :]
        b_last = b_col[lc - 1:lc, :]
        w_s = jnp.exp(b_last - b_col + i_col - m_new)
        dec = jnp.exp(b_last + m_st - m_new)
        c_ref[h] = dec * c_st + _mm_tn(k * w_s, vext)
        m_ref[h:h + 1, :] = jnp.broadcast_to(m_new, (1, LANES))
        yf = hc * lax.rsqrt(jnp.mean(hc * hc, axis=-1, keepdims=True) + NORM_EPS)
        og = og_ref[:, h * dv:(h + 1) * dv].astype(F32)
        o_ref[:, h * dv:(h + 1) * dv] = (_sigmoid(og) * (yf * hn_ref[:, h * dv:(h + 1) * dv])).astype(BF16)


def _mlscan(proj, gates, bif, hn_w, batch, seq, d, lc=256):
    t = proj.shape[0]
    nc = seq // lc
    dq = d // 2
    tok = lambda bi, ci: bi * nc + ci
    return pl.pallas_call(
        _mlscan_kernel,
        grid=(batch, nc),
        in_specs=[
            pl.BlockSpec((lc, dq), lambda bi, ci: (tok(bi, ci), 0)),
            pl.BlockSpec((lc, dq), lambda bi, ci: (tok(bi, ci), 1)),
            pl.BlockSpec((lc, d), lambda bi, ci: (tok(bi, ci), 1)),
            pl.BlockSpec((lc, d), lambda bi, ci: (tok(bi, ci), 2)),
            pl.BlockSpec((lc, LANES), lambda bi, ci: (tok(bi, ci), 0)),
            pl.BlockSpec((1, LANES), lambda bi, ci: (0, 0)),
            pl.BlockSpec((1, d), lambda bi, ci: (0, 0)),
        ],
        out_specs=pl.BlockSpec((lc, d), lambda bi, ci: (tok(bi, ci), 0)),
        out_shape=jax.ShapeDtypeStruct((t, d), BF16),
        scratch_shapes=[pltpu.VMEM((ML_HEADS, dq // ML_HEADS, d // ML_HEADS + LANES), F32),
                        pltpu.VMEM((8, LANES), F32)],
        compiler_params=_params("parallel", "arbitrary"),
        name="mlscan",
    )(proj, proj, proj, proj, gates, bif, hn_w)


def _pad_cols(w, n):
    return jnp.pad(w, ((0, 0), (0, n - w.shape[1])))


def _pad_rows(w, n):
    return jnp.pad(w, ((0, n - w.shape[0]), (0, 0)))


def kernel(x, positions, norm_mix, norm_ffn, norm_final, rw_mu, rw_w_rkv, rw_w0, rw_w1, rw_w2, rw_a0, rw_a1, rw_a2, rw_g1, rw_g2, rw_k_k, rw_k_a, rw_r_k, rw_ln_w, rw_ln_b, rw_w_o, rw_v0, rw_v1, rw_v2, ret_w_in, ret_gn_w, ret_w_out, ml_w_in, ml_b_if, ml_hn_w, ml_w_out, ffn_w1, ffn_w3, ffn_w2):
    batch, seq, d = x.shape
    t = batch * seq
    depth = norm_mix.shape[0]
    xf = x.reshape(t, d)
    row = lambda vec: vec.reshape(1, -1).astype(F32)

    dk = d // RET_HEADS
    inv_freq = (1.0 / (ROPE_BASE ** jnp.linspace(0.0, 1.0, dk // 2, dtype=F32))).reshape(1, -1)
    cos, sin = _rope_tables(positions.reshape(t, 1), inv_freq)
    log_gamma = jnp.log1p(-jnp.exp2(-5.0 - jnp.arange(RET_HEADS, dtype=F32)))
    lg = jnp.broadcast_to(log_gamma[:, None, None], (RET_HEADS, 1, LANES))

    ffn_w = (ffn_w1[0].astype(BF16), ffn_w3[0].astype(BF16), ffn_w2[0].astype(BF16))
    rw_w_rkv_b = rw_w_rkv.astype(BF16)

    v_first = None
    for i in range(depth):
        kind, j = i % 3, i // 3
        g_mix = row(norm_mix[i])
        if kind == 0:
            has_vres = j > 0
            zeros_v1 = jnp.zeros((d, RW_LORA_G - RW_LORA_V), F32)
            zeros_v2 = jnp.zeros((RW_LORA_G - RW_LORA_V, d), F32)
            lin = jnp.concatenate([
                _pad_cols(rw_w1[j], RW_LORA_A - RW_LORA_W), _pad_cols(rw_a1[j], RW_LORA_V - RW_LORA_A),
                _pad_cols(rw_v1[j - 1], RW_LORA_G - RW_LORA_V) if has_vres else zeros_v1,
                rw_g1[j]], axis=1).astype(BF16)
            lout = jnp.concatenate([
                _pad_rows(rw_w2[j], RW_LORA_A - RW_LORA_W), _pad_rows(rw_a2[j], RW_LORA_V - RW_LORA_A),
                _pad_rows(rw_v2[j - 1], RW_LORA_G - RW_LORA_V) if has_vres else zeros_v2,
                rw_g2[j]], axis=0).astype(BF16)
            v0 = rw_v0[j - 1] if has_vres else jnp.zeros((d,), F32)
            zero = jnp.zeros((d,), F32)
            vecs = jnp.stack([rw_w0[j], rw_a0[j], v0, rw_k_k[j], rw_k_a[j], zero, zero, zero]).astype(F32)
            r, lw, k, v, a, b, g = _rwproj(xf, g_mix, rw_mu[j].astype(F32), rw_w_rkv_b, j, lin, lout,
                                           vecs, v_first if has_vres else None, seq)
            if j == 0:
                v_first = v
            svecs = jnp.stack([rw_r_k[j].reshape(-1), rw_ln_w[j], rw_ln_b[j], zero, zero, zero, zero, zero]).astype(F32)
            z = _rwscan(r, lw, k, v, a, b, g, svecs, batch, seq)
            xf = _outproj(z, rw_w_o[j].astype(BF16), xf)
        elif kind == 1:
            (proj,) = _normproj(xf, g_mix, ret_w_in[j].astype(BF16), tn=1024)
            z = _retscan(proj, cos, sin, lg, row(ret_gn_w[j]), batch, seq, d)
            xf = _outproj(z, ret_w_out[j].astype(BF16), xf)
        else:
            n_pad = 3 * d + 2 * LANES
            tn = n_pad // 5
            w_in = _pad_cols(ml_w_in[j], n_pad).astype(BF16)
            gate_lo = 3 * d - (n_pad - tn)
            proj, gates = _normproj(xf, g_mix, w_in, tn=tn, side_cols=(gate_lo, gate_lo + LANES))
            bif = _pad_cols(ml_b_if[j].reshape(1, -1).astype(F32), LANES)
            z = _mlscan(proj, gates, bif, row(ml_hn_w[j]), batch, seq, d)
            xf = _outproj(z, ml_w_out[j].astype(BF16), xf)
        nxt = (ffn_w1, ffn_w3, ffn_w2, i + 1) if i + 1 < depth else None
        xf, *ffn_w = _ffn(xf, row(norm_ffn[i]), *ffn_w, row(norm_final), final_norm=(i == depth - 1),
                          next_weights=nxt)
    return xf.reshape(batch, seq, d)
```

```python
import functools

import jax
import jax.numpy as jnp
from jax import lax
from jax.experimental import pallas as pl
from jax.experimental.pallas import tpu as pltpu

F32 = jnp.float32
BF16 = jnp.bfloat16

NORM_EPS = 1e-6
RW_HEAD = 64
RW_GN_EPS = 64e-5
RW_CHUNK = 64
RW_INV_BASE = 8
RET_HEADS = 8
ROPE_BASE = 10000.0
ML_HEADS = 4
ML_GATE_CAP = 15.0
LANES = 128
SUBLANES = 8
NORM_ROWS = 256
VMEM_LIMIT = 56 * 1024 * 1024


def _params(*sem):
    return pltpu.CompilerParams(dimension_semantics=sem, vmem_limit_bytes=VMEM_LIMIT)


def _mm(a, b):
    return jnp.dot(a.astype(BF16), b.astype(BF16), preferred_element_type=F32)


def _mm_nt(a, b):
    return lax.dot_general(a.astype(BF16), b.astype(BF16), (((1,), (1,)), ((), ())),
                           preferred_element_type=F32)


def _mm_tn(a, b):
    return lax.dot_general(a.astype(BF16), b.astype(BF16), (((0,), (0,)), ((), ())),
                           preferred_element_type=F32)


def _rms(x, g):
    return x * lax.rsqrt(jnp.mean(x * x, axis=-1, keepdims=True) + NORM_EPS) * g


def _sigmoid(x):
    return 0.5 * jnp.tanh(0.5 * x) + 0.5


def _softplus(x):
    return jnp.maximum(x, 0.0) + jnp.log(1.0 + jnp.exp(-jnp.abs(x)))


def _cumsum_rows(x):
    n = x.shape[0]
    row = lax.broadcasted_iota(jnp.int32, x.shape, 0)
    s = 1
    while s < n:
        x = x + jnp.where(row >= s, pltpu.roll(x, s, 0), 0.0)
        s *= 2
    return x


def _seg64_sum(x, split=True):
    r = lax.broadcasted_iota(jnp.int32, (LANES, LANES), 0) // RW_HEAD
    c = lax.broadcasted_iota(jnp.int32, (LANES, LANES), 1) // RW_HEAD
    bd = jnp.where(r == c, 1.0, 0.0).astype(BF16)
    hi = x.astype(BF16)
    out = jnp.dot(hi, bd, preferred_element_type=F32)
    if split:
        out = out + jnp.dot((x - hi.astype(F32)).astype(BF16), bd, preferred_element_type=F32)
    return out


def _ffn_kernel(*refs, final_norm, cast_next):
    if cast_next:
        (x_ref, g_ref, w1_ref, w3_ref, w2_ref, gf_ref, n1_ref, n3_ref, n2_ref,
         o_ref, n1_o, n3_o, n2_o, h_s) = refs
        n1_o[...] = n1_ref[...].astype(BF16)
        n3_o[...] = n3_ref[...].astype(BF16)
        n2_o[...] = n2_ref[...].astype(BF16)
    else:
        x_ref, g_ref, w1_ref, w3_ref, w2_ref, gf_ref, o_ref, h_s = refs
    j = pl.program_id(1)
    tm = x_ref.shape[0]
    half = w1_ref.shape[1] // 2

    @pl.when(j == 0)
    def _():
        for s in range(0, tm, NORM_ROWS):
            x = x_ref[s:s + NORM_ROWS, :]
            h_s[s:s + NORM_ROWS, :] = _rms(x, g_ref[...]).astype(BF16)
            o_ref[s:s + NORM_ROWS, :] = x

    h = h_s[...]
    gate_up = [(jnp.dot(h, w1_ref[:, c:c + half], preferred_element_type=F32),
                jnp.dot(h, w3_ref[:, c:c + half], preferred_element_type=F32)) for c in (0, half)]
    down = None
    for n, (a, b) in enumerate(gate_up):
        act = (a * _sigmoid(a) * b).astype(BF16)
        part = jnp.dot(act, w2_ref[n * half:(n + 1) * half, :], preferred_element_type=F32)
        down = part if down is None else down + part
    o_ref[...] += down

    if final_norm:
        @pl.when(j == pl.num_programs(1) - 1)
        def _():
            for s in range(0, tm, NORM_ROWS):
                o_ref[s:s + NORM_ROWS, :] = _rms(o_ref[s:s + NORM_ROWS, :], gf_ref[...])


def _ffn(x, g, w1, w3, w2, gf, final_norm, next_weights=None, tm=1024, tf=512):
    t, d = x.shape
    ff = w1.shape[1]
    tf = min(tf, ff)
    ni = t // tm
    in_specs = [
        pl.BlockSpec((tm, d), lambda i, j: (i, 0), pipeline_mode=pl.Buffered(1)),
        pl.BlockSpec((1, d), lambda i, j: (0, 0)),
        pl.BlockSpec((d, tf), lambda i, j: (0, j)),
        pl.BlockSpec((d, tf), lambda i, j: (0, j)),
        pl.BlockSpec((tf, d), lambda i, j: (j, 0)),
        pl.BlockSpec((1, d), lambda i, j: (0, 0)),
    ]
    out_specs = [pl.BlockSpec((tm, d), lambda i, j: (i, 0))]
    out_shape = [jax.ShapeDtypeStruct((t, d), F32)]
    args = [x, g, w1, w3, w2, gf]
    if next_weights is not None:
        w1s, w3s, w2s, layer = next_weights
        rb = d // ni
        assert rb * ni == d and rb % LANES == 0
        in_specs += [pl.BlockSpec((None, rb, tf), lambda i, j: (layer, i, j)),
                     pl.BlockSpec((None, rb, tf), lambda i, j: (layer, i, j)),
                     pl.BlockSpec((None, tf, rb), lambda i, j: (layer, j, i))]
        out_specs += [pl.BlockSpec((rb, tf), lambda i, j: (i, j)),
                      pl.BlockSpec((rb, tf), lambda i, j: (i, j)),
                      pl.BlockSpec((tf, rb), lambda i, j: (j, i))]
        out_shape += [jax.ShapeDtypeStruct((d, ff), BF16), jax.ShapeDtypeStruct((d, ff), BF16),
                      jax.ShapeDtypeStruct((ff, d), BF16)]
        args += [w1s, w3s, w2s]
    return pl.pallas_call(
        functools.partial(_ffn_kernel, final_norm=final_norm, cast_next=next_weights is not None),
        grid=(ni, ff // tf),
        in_specs=in_specs,
        out_specs=out_specs,
        out_shape=out_shape,
        scratch_shapes=[pltpu.VMEM((tm, d), BF16)],
        compiler_params=_params("parallel", "arbitrary"),
        name="ffn",
    )(*args)


def _outproj_kernel(z_ref, w_ref, x_ref, o_ref):
    o_ref[...] = x_ref[...] + jnp.dot(z_ref[...], w_ref[...], preferred_element_type=F32)


OUTPROJ_W_BYTES = 8 * 1024 * 1024


def _outproj(z, w, x, tm=512):
    t, k = z.shape
    d = w.shape[1]
    tn = min(d, OUTPROJ_W_BYTES // (2 * k))
    return pl.pallas_call(
        _outproj_kernel,
        grid=(d // tn, t // tm),
        in_specs=[
            pl.BlockSpec((tm, k), lambda j, i: (i, 0)),
            pl.BlockSpec((k, tn), lambda j, i: (0, j)),
            pl.BlockSpec((tm, tn), lambda j, i: (i, j)),
        ],
        out_specs=pl.BlockSpec((tm, tn), lambda j, i: (i, j)),
        out_shape=jax.ShapeDtypeStruct((t, d), F32),
        compiler_params=_params("arbitrary", "arbitrary"),
        name="outproj",
    )(z, w, x)


def _normproj_kernel(x_ref, g_ref, w_ref, o_ref, *rest, side_cols):
    h_s = rest[-1]

    @pl.when(pl.program_id(1) == 0)
    def _():
        for s in range(0, x_ref.shape[0], NORM_ROWS):
            h_s[s:s + NORM_ROWS, :] = _rms(x_ref[s:s + NORM_ROWS, :], g_ref[...]).astype(BF16)

    acc = jnp.dot(h_s[...], w_ref[...], preferred_element_type=F32)
    o_ref[...] = acc.astype(o_ref.dtype)
    if side_cols:
        @pl.when(pl.program_id(1) == pl.num_programs(1) - 1)
        def _():
            rest[0][...] = acc[:, side_cols[0]:side_cols[1]]


def _normproj(x, g, w, tn, side_cols=None, tm=1024):
    t, d = x.shape
    n = w.shape[1]
    out_specs = [pl.BlockSpec((tm, tn), lambda i, j: (i, j))]
    out_shape = [jax.ShapeDtypeStruct((t, n), BF16)]
    if side_cols:
        width = side_cols[1] - side_cols[0]
        out_specs.append(pl.BlockSpec((tm, width), lambda i, j: (i, 0)))
        out_shape.append(jax.ShapeDtypeStruct((t, width), F32))
    return pl.pallas_call(
        functools.partial(_normproj_kernel, side_cols=side_cols),
        grid=(t // tm, n // tn),
        in_specs=[
            pl.BlockSpec((tm, d), lambda i, j: (i, 0)),
            pl.BlockSpec((1, d), lambda i, j: (0, 0)),
            pl.BlockSpec((d, tn), lambda i, j: (0, j)),
        ],
        out_specs=out_specs,
        out_shape=out_shape,
        scratch_shapes=[pltpu.VMEM((tm, d), BF16)],
        compiler_params=_params("parallel", "arbitrary"),
        name="normproj",
    )(x, g, w)


RW_LORA_W = 0
RW_LORA_A = 128
RW_LORA_V = 256
RW_LORA_G = 384
RW_LORA_END = 640
RW_PROLOGUE_ROWS = 128


def _rwproj_kernel(*refs, tiles_per_seq, has_vres):
    if has_vres:
        (x_ref, xp_ref, g_ref, mu_ref, wrkv_ref, lin_ref, lout_ref, vec_ref, vf_ref,
         r_o, lw_o, k_o, v_o, a_o, b_o, g_o, xr_s, xk_s, xv_s, l1_s) = refs
    else:
        (x_ref, xp_ref, g_ref, mu_ref, wrkv_ref, lin_ref, lout_ref, vec_ref,
         r_o, lw_o, k_o, v_o, a_o, b_o, g_o, xr_s, xk_s, xv_s, l1_s) = refs
        vf_ref = None
    i = pl.program_id(0)

    @pl.when(pl.program_id(1) == 0)
    def _():
        g = g_ref[...]
        mu = mu_ref[...]
        sub = RW_PROLOGUE_ROWS
        for s in range(x_ref.shape[0] // sub):
            rows = slice(s * sub, (s + 1) * sub)
            h = _rms(x_ref[rows, :], g)
            if s == 0:
                hp = _rms(xp_ref[SUBLANES - 1:SUBLANES, :], g)
                hp = jnp.where(i % tiles_per_seq == 0, 0.0, hp)
            else:
                hp = _rms(x_ref[s * sub - SUBLANES:s * sub, :], g)[SUBLANES - 1:SUBLANES, :]
            row = lax.broadcasted_iota(jnp.int32, h.shape, 0)
            xx = jnp.where(row == 0, hp, pltpu.roll(h, 1, 0)) - h
            xr_s[rows, :] = (h + xx * mu[0:1]).astype(BF16)
            xk_s[rows, :] = (h + xx * mu[2:3]).astype(BF16)
            xv = (h + xx * mu[3:4]).astype(BF16)
            xv_s[rows, :] = xv
            xw = h + xx * mu[1:2]
            xa = h + xx * mu[4:5]
            xg = h + xx * mu[5:6]
            l1_s[rows, RW_LORA_W:RW_LORA_A] = jnp.tanh(_mm(xw, lin_ref[:, RW_LORA_W:RW_LORA_A])).astype(BF16)
            l1_s[rows, RW_LORA_A:RW_LORA_V] = _mm(xa, lin_ref[:, RW_LORA_A:RW_LORA_V]).astype(BF16)
            l1_s[rows, RW_LORA_V:RW_LORA_G] = _mm(xv, lin_ref[:, RW_LORA_V:RW_LORA_G]).astype(BF16)
            l1_s[rows, RW_LORA_G:RW_LORA_END] = _sigmoid(
                _mm(xg, lin_ref[:, RW_LORA_G:RW_LORA_END])).astype(BF16)

    r = jnp.dot(xr_s[...], wrkv_ref[0], preferred_element_type=F32)
    k = jnp.dot(xk_s[...], wrkv_ref[1], preferred_element_type=F32)
    v = jnp.dot(xv_s[...], wrkv_ref[2], preferred_element_type=F32)
    vec = vec_ref[...]
    w0, a0, v0, k_k, k_a = (vec[n:n + 1] for n in range(5))

    def lora2(lo, hi):
        return jnp.dot(l1_s[:, lo:hi], lout_ref[lo:hi, :], preferred_element_type=F32)

    w_log = -_softplus(-(w0 + lora2(RW_LORA_W, RW_LORA_A))) - 0.5
    lw_o[...] = -jnp.exp(w_log)
    a = _sigmoid(a0 + lora2(RW_LORA_A, RW_LORA_V))
    if has_vres:
        v = v + (vf_ref[...].astype(F32) - v) * _sigmoid(v0 + lora2(RW_LORA_V, RW_LORA_G))
    g_o[...] = lora2(RW_LORA_G, RW_LORA_END).astype(g_o.dtype)
    kk = k * k_k
    tn = kk.shape[1]
    ss = jnp.concatenate([_seg64_sum(jnp.square(kk[:, c:c + LANES]), split=False) for c in range(0, tn, LANES)],
                         axis=1)
    kk = kk * lax.rsqrt(jnp.maximum(ss, 1e-24))
    r_o[...] = r.astype(r_o.dtype)
    k_o[...] = (k * (1.0 + (a - 1.0) * k_a)).astype(k_o.dtype)
    v_o[...] = v.astype(v_o.dtype)
    a_o[...] = (-kk).astype(a_o.dtype)
    b_o[...] = (kk * a).astype(b_o.dtype)


def _rwproj(x, g, mu, wrkv, layer, lin, lout, vecs, vfirst, seq, tm=512, tn=512):
    t, d = x.shape
    has_vres = vfirst is not None
    rows8 = tm // SUBLANES
    in_specs = [
        pl.BlockSpec((tm, d), lambda i, j: (i, 0)),
        pl.BlockSpec((SUBLANES, d), lambda i, j: (jnp.maximum(i * rows8 - 1, 0), 0)),
        pl.BlockSpec((1, d), lambda i, j: (0, 0)),
        pl.BlockSpec((6, d), lambda i, j: (0, 0)),
        pl.BlockSpec((None, 3, d, tn), lambda i, j: (layer, 0, 0, j)),
        pl.BlockSpec((d, RW_LORA_END), lambda i, j: (0, 0)),
        pl.BlockSpec((RW_LORA_END, tn), lambda i, j: (0, j)),
        pl.BlockSpec((SUBLANES, tn), lambda i, j: (0, j)),
    ]
    args = [x, x, g, mu, wrkv, lin, lout, vecs]
    if has_vres:
        in_specs.append(pl.BlockSpec((tm, tn), lambda i, j: (i, j)))
        args.append(vfirst)
    out_dtypes = [BF16, F32, BF16, BF16, BF16, BF16, BF16]
    return pl.pallas_call(
        functools.partial(_rwproj_kernel, tiles_per_seq=seq // tm, has_vres=has_vres),
        grid=(t // tm, d // tn),
        in_specs=in_specs,
        out_specs=[pl.BlockSpec((tm, tn), lambda i, j: (i, j))] * 7,
        out_shape=[jax.ShapeDtypeStruct((t, d), dt) for dt in out_dtypes],
        scratch_shapes=[pltpu.VMEM((tm, d), BF16)] * 3 + [pltpu.VMEM((tm, RW_LORA_END), BF16)],
        compiler_params=_params("parallel", "arbitrary"),
        name="rwproj",
    )(*args)


def _rwscan_kernel(r_ref, lw_ref, k_ref, v_ref, a_ref, b_ref, g_ref, vec_ref, o_ref,
                   s_ref, rp_st, yp_st, mt_st, ct_st, pl_st, bg_st, g_st, *, chunks):
    cl = RW_CHUNK
    pair = 2 * cl
    pairs = lw_ref.shape[1] // LANES
    ti = pl.program_id(2)
    slot_w = ti % 2
    slot_r = 1 - slot_w

    @pl.when(ti == 0)
    def _():
        s_ref[...] = jnp.zeros_like(s_ref)
        for ref in (rp_st, yp_st, mt_st, ct_st, pl_st, bg_st, g_st):
            ref[1] = jnp.zeros(ref.shape[1:], ref.dtype)

    head0 = lax.broadcasted_iota(jnp.int32, (cl, LANES), 1) < RW_HEAD

    def stack(x):
        return jnp.concatenate([jnp.where(head0, x, 0.0), jnp.where(head0, 0.0, x)], axis=0).astype(BF16)

    vec = vec_ref[...]
    states = [s_ref[p] for p in range(pairs)]
    ys = [[] for _ in range(pairs)]
    links_done = [0]

    def serial_link():
        j = links_done[0]
        if j >= chunks:
            return
        links_done[0] = j + 1
        for p in range(pairs):
            c = p * chunks + j
            st = states[p]
            ys[p].append(_mm_nt(rp_st[slot_r, c], stack(st)) + yp_st[slot_r, c])
            states[p] = jnp.exp(pl_st[slot_r, c][0:1]) * st + _mm(st, mt_st[slot_r, c]) + ct_st[slot_r, c]

    norm = {}

    def finish_mean():
        while links_done[0] < chunks:
            serial_link()
        for p in range(pairs):
            s_ref[p] = states[p]
        norm["y"] = [jnp.concatenate(ys[p], axis=0) for p in range(pairs)]
        norm["mean"] = [_seg64_sum(y) * (1.0 / RW_HEAD) for y in norm["y"]]

    def finish_var():
        norm["yc"] = [norm["y"][p] - norm["mean"][p] for p in range(pairs)]
        norm["var"] = [_seg64_sum(yc * yc, split=False) * (1.0 / RW_HEAD) for yc in norm["yc"]]

    def finish_store():
        for p in range(pairs):
            lanes = slice(p * LANES, (p + 1) * LANES)
            yn = norm["yc"][p] * lax.rsqrt(norm["var"][p] + RW_GN_EPS) * vec[1:2, lanes] + vec[2:3, lanes]
            o_ref[:, lanes] = (yn * g_st[slot_r, :, lanes] + bg_st[slot_r, :, lanes]).astype(BF16)

    pending = [finish_mean, finish_var, finish_store]

    def advance():
        if links_done[0] < chunks:
            serial_link()
        elif pending:
            pending.pop(0)()

    advance()

    ri = lax.broadcasted_iota(jnp.int32, (cl, LANES), 0)
    ci = lax.broadcasted_iota(jnp.int32, (cl, LANES), 1) % cl
    strict = ri > ci
    incl = ri >= ci
    eye = jnp.where(ri == ci, 1.0, 0.0)
    rb = lax.broadcasted_iota(jnp.int32, (pair, LANES), 0) // cl
    cb = lax.broadcasted_iota(jnp.int32, (pair, LANES), 1) // RW_HEAD
    same_head = rb == cb

    cs = range(pairs * chunks)
    rows = [(slice((c % chunks) * cl, (c % chunks + 1) * cl),
             slice((c // chunks) * LANES, (c // chunks + 1) * LANES)) for c in cs]

    def prepare(c):
        tok, lanes = rows[c]
        lw = lw_ref[tok, lanes]
        cum = lw
        s = 1
        while s < cl:
            cum = cum + jnp.where(ri >= s, pltpu.roll(cum, s, 0), 0.0)
            s *= 2
        r, k, v = (ref[tok, lanes].astype(F32) for ref in (r_ref, k_ref, v_ref))
        a, b, g = (ref[tok, lanes].astype(F32) for ref in (a_ref, b_ref, g_ref))
        bonus = _seg64_sum(r * k * vec[0:1, lanes], split=False)
        bg_st[slot_w, tok, lanes] = bonus * v * g
        g_st[slot_w, tok, lanes] = g
        last = cum[cl - 1:cl, :]
        e_neg = jnp.exp(-cum)
        tail = jnp.exp(last - cum)
        at = a * jnp.exp(cum - lw)
        rt = r * jnp.exp(cum)
        gram = _mm_nt(jnp.concatenate([at, rt], axis=0),
                      jnp.concatenate([stack(b * e_neg), stack(k * e_neg)], axis=0))
        return dict(at=at, rt=rt, last=last, gram=gram, v_c=v.astype(BF16), v_s=stack(v),
                    bl_c=(b * tail).astype(BF16), kl_c=(k * tail).astype(BF16))

    items = []
    for c in cs:
        items.append(prepare(c))
        if c % 4 == 3:
            advance()
    at, rt, last, gram = ([it[name] for it in items] for name in ("at", "rt", "last", "gram"))
    v_c, v_s, bl_c, kl_c = ([it[name] for it in items] for name in ("v_c", "v_s", "bl_c", "kl_c"))
    n_ab = [jnp.where(strict, gram[c][:cl, :LANES], 0.0) for c in cs]
    a_ak = [jnp.where(strict, gram[c][:cl, LANES:], 0.0) for c in cs]
    a_r = [jnp.concatenate([jnp.where(incl, gram[c][cl:, :LANES], 0.0),
                            jnp.where(incl, gram[c][cl:, LANES:], 0.0)], axis=1).astype(BF16) for c in cs]
    x1 = [_mm(a_ak[c], v_s[c]) for c in cs]
    advance()
    blk_r, blk_c = ri // RW_INV_BASE, ci // RW_INV_BASE
    n_d = [jnp.where(blk_r == blk_c, n_ab[c], 0.0) for c in cs]
    tinv = [eye + n_d[c] for c in cs]
    pw = [_mm(n_d[c], stack(n_d[c])) for c in cs]
    advance()
    span = 4
    while span < RW_INV_BASE:
        both = [_mm(jnp.concatenate([pw[c], tinv[c]], axis=0), stack(pw[c])) for c in cs]
        pw = [both[c][:cl] for c in cs]
        tinv = [tinv[c] + both[c][cl:] for c in cs]
        span *= 2
        advance()
    tinv = [tinv[c] + _mm(tinv[c], stack(pw[c])) for c in cs]
    size = RW_INV_BASE
    while size < cl:
        lower_left = ((ri // (2 * size)) == (ci // (2 * size))) & ((ri // size) == (ci // size) + 1)
        advance()
        e = [_mm(jnp.where(lower_left, n_ab[c], 0.0), stack(tinv[c])) for c in cs]
        tinv = [tinv[c] + _mm(tinv[c], stack(e[c])) for c in cs]
        size *= 2
    advance()
    tz = [_mm(tinv[c], jnp.concatenate([stack(at[c]), stack(x1[c])], axis=1)) for c in cs]
    advance()
    zeros = jnp.zeros((pair, LANES), BF16)
    q = [_mm(a_r[c], jnp.concatenate(
        [jnp.concatenate([stack(tz[c][:, :LANES]), stack(tz[c][:, LANES:])], axis=1),
         jnp.concatenate([zeros, v_s[c]], axis=1)], axis=0)) for c in cs]
    while links_done[0] < chunks or pending:
        advance()
    mt =[jnp.where(same_head, _mm_tn(tz[c][:, :LANES], bl_c[c]), 0.0) for c in cs]
    ct = [_mm_tn(jnp.concatenate([tz[c][:, LANES:].astype(BF16), v_c[c]], axis=0),
                 jnp.concatenate([bl_c[c], kl_c[c]], axis=0)) for c in cs]

    for c in cs:
        rp_st[slot_w, c] = (rt[c] + q[c][:, :LANES]).astype(BF16)
        yp_st[slot_w, c] = q[c][:, LANES:]
        mt_st[slot_w, c] = mt[c].astype(BF16)
        ct_st[slot_w, c] = jnp.where(head0, ct[c][:cl], ct[c][cl:])
        pl_st[slot_w, c] = jnp.broadcast_to(last[c], (SUBLANES, LANES))


def _rwscan(r, lw, k, v, a, b, g, vecs, batch, seq, lt=256, pairs=4):
    t, d = r.shape
    nt = seq // lt
    width = pairs * LANES
    items = pairs * (lt // RW_CHUNK)
    blk_in = pl.BlockSpec((lt, width), lambda bi, hp, ti: (bi * nt + jnp.minimum(ti, nt - 1), hp))
    blk_out = pl.BlockSpec((lt, width), lambda bi, hp, ti: (bi * nt + jnp.maximum(ti - 1, 0), hp))
    return pl.pallas_call(
        functools.partial(_rwscan_kernel, chunks=lt // RW_CHUNK),
        grid=(batch, d // width, nt + 1),
        in_specs=[blk_in] * 7 + [pl.BlockSpec((SUBLANES, width), lambda bi, hp, ti: (0, hp))],
        out_specs=blk_out,
        out_shape=jax.ShapeDtypeStruct((t, d), BF16),
        scratch_shapes=[
            pltpu.VMEM((pairs, RW_HEAD, LANES), F32),
            pltpu.VMEM((2, items, RW_CHUNK, LANES), BF16),
            pltpu.VMEM((2, items, RW_CHUNK, LANES), F32),
            pltpu.VMEM((2, items, LANES, LANES), BF16),
            pltpu.VMEM((2, items, RW_HEAD, LANES), F32),
            pltpu.VMEM((2, items, 8, LANES), F32),
            pltpu.VMEM((2, lt, width), F32),
            pltpu.VMEM((2, lt, width), F32),
        ],
        compiler_params=_params("parallel", "parallel", "arbitrary"),
        name="rwscan",
    )(r, lw, k, v, a, b, g, vecs)


def _rope_kernel(pos_ref, inv_ref, cos_ref, sin_ref):
    ang = pos_ref[...].astype(F32) * inv_ref[...]
    cos_ref[...] = jnp.cos(ang)
    sin_ref[...] = jnp.sin(ang)


def _rope_tables(pos, inv_freq, tm=512):
    t = pos.shape[0]
    half = inv_freq.shape[1]
    out = jax.ShapeDtypeStruct((t, half), F32)
    return pl.pallas_call(
        _rope_kernel,
        grid=(t // tm,),
        in_specs=[pl.BlockSpec((tm, 1), lambda i: (i, 0)), pl.BlockSpec((1, half), lambda i: (0, 0))],
        out_specs=[pl.BlockSpec((tm, half), lambda i: (i, 0))] * 2,
        out_shape=[out, out],
        compiler_params=_params("parallel"),
        name="rope",
    )(pos, inv_freq)


RET_HEADS_PER_STEP = 4


def _retscan_kernel(q_ref, k_ref, v_ref, gate_ref, cos_ref, sin_ref, lg_ref, gn_ref, o_ref, st_ref, intra_ref):
    lc = q_ref.shape[0]
    heads = lg_ref.shape[0]
    dk = q_ref.shape[1] // heads
    dv = v_ref.shape[1] // heads
    half = dk // 2
    ic = lax.broadcasted_iota(jnp.int32, (lc, 1), 0).astype(F32)
    lgs = [lg_ref[h][:, 0:1] for h in range(heads)]

    @pl.when(pl.program_id(2) == 0)
    def _():
        st_ref[...] = jnp.zeros_like(st_ref)
        diff = ic - lax.broadcasted_iota(jnp.int32, (1, lc), 1).astype(F32)
        causal = diff >= 0
        for h in range(heads):
            intra_ref[h] = jnp.where(causal, jnp.exp(lgs[h] * jnp.where(causal, diff, 0.0)), 0.0)

    cos, sin = cos_ref[...], sin_ref[...]

    def rot(t):
        t1, t2 = t[:, :half], t[:, half:]
        return jnp.concatenate([t1 * cos - t2 * sin, t1 * sin + t2 * cos], axis=1)

    for h in range(heads):
        lg = lgs[h]
        q = rot(q_ref[:, h * dk:(h + 1) * dk].astype(F32))
        k = rot(k_ref[:, h * dk:(h + 1) * dk].astype(F32)) * (dk ** -0.5)
        v = v_ref[:, h * dv:(h + 1) * dv]
        xi = jnp.exp(lg * (ic + 1.0))
        zeta = jnp.exp(lg * (lc - 1.0 - ic))
        st = st_ref[h]
        sc = _mm_nt(q, k) * intra_ref[h]
        y = _mm(sc, v) + _mm(q, st) * xi
        st_ref[h] = jnp.exp(lg * lc) * st + _mm_tn(k * zeta, v)
        mean = jnp.mean(y, axis=-1, keepdims=True)
        yc = y - mean
        yn = yc * lax.rsqrt(jnp.mean(yc * yc, axis=-1, keepdims=True) + NORM_EPS) * gn_ref[:, h * dv:(h + 1) * dv]
        gate = gate_ref[:, h * dv:(h + 1) * dv].astype(F32)
        o_ref[:, h * dv:(h + 1) * dv] = (gate * _sigmoid(gate) * yn).astype(BF16)


def _retscan(proj, cos, sin, lg, gn_w, batch, seq, d, lc=256):
    t = proj.shape[0]
    nc = seq // lc
    hps = RET_HEADS_PER_STEP
    nhb = RET_HEADS // hps
    dk, dv = d // RET_HEADS, 2 * d // RET_HEADS
    tok = lambda bi, hb, ci: bi * nc + ci
    return pl.pallas_call(
        _retscan_kernel,
        grid=(batch, nhb, nc),
        in_specs=[
            pl.BlockSpec((lc, hps * dk), lambda bi, hb, ci: (tok(bi, hb, ci), hb)),
            pl.BlockSpec((lc, hps * dk), lambda bi, hb, ci: (tok(bi, hb, ci), nhb + hb)),
            pl.BlockSpec((lc, hps * dv), lambda bi, hb, ci: (tok(bi, hb, ci), nhb + hb)),
            pl.BlockSpec((lc, hps * dv), lambda bi, hb, ci: (tok(bi, hb, ci), 2 * nhb + hb)),
            pl.BlockSpec((lc, dk // 2), lambda bi, hb, ci: (tok(bi, hb, ci), 0)),
            pl.BlockSpec((lc, dk // 2), lambda bi, hb, ci: (tok(bi, hb, ci), 0)),
            pl.BlockSpec((hps, 1, LANES), lambda bi, hb, ci: (hb, 0, 0)),
            pl.BlockSpec((1, hps * dv), lambda bi, hb, ci: (0, hb)),
        ],
        out_specs=pl.BlockSpec((lc, hps * dv), lambda bi, hb, ci: (tok(bi, hb, ci), hb)),
        out_shape=jax.ShapeDtypeStruct((t, 2 * d), BF16),
        scratch_shapes=[pltpu.VMEM((hps, dk, dv), F32), pltpu.VMEM((hps, lc, lc), F32)],
        compiler_params=_params("parallel", "parallel", "arbitrary"),
        name="retscan",
    )(proj, proj, proj, proj, cos, sin, lg, gn_w)


def _mlscan_kernel(q_ref, k_ref, v_ref, og_ref, gt_ref, bif_ref, hn_ref, o_ref, c_ref, m_ref):
    lc = q_ref.shape[0]
    dqk = q_ref.shape[1] // ML_HEADS
    dv = v_ref.shape[1] // ML_HEADS

    @pl.when(pl.program_id(1) == 0)
    def _():
        c_ref[...] = jnp.zeros_like(c_ref)
        m_ref[...] = jnp.zeros_like(m_ref)

    gt = gt_ref[...] + bif_ref[...]
    gt = ML_GATE_CAP * jnp.tanh(gt * (1.0 / ML_GATE_CAP))
    lane = lax.broadcasted_iota(jnp.int32, gt.shape, 1)
    is_f = (lane >= ML_HEADS) & (lane < 2 * ML_HEADS)
    log_f = jnp.where(is_f, -_softplus(-gt), 0.0)
    bcum = pltpu.roll(_cumsum_rows(log_f), LANES - ML_HEADS, 1)
    rj_t = (bcum - gt).T
    ir = lax.broadcasted_iota(jnp.int32, (lc, lc), 0)
    ic = lax.broadcasted_iota(jnp.int32, (lc, lc), 1)
    causal = ir >= ic
    ones_col = jnp.where(lax.broadcasted_iota(jnp.int32, (lc, LANES), 1) == 0, 1.0, 0.0).astype(BF16)
    for h in range(ML_HEADS):
        q = q_ref[:, h * dqk:(h + 1) * dqk]
        k = k_ref[:, h * dqk:(h + 1) * dqk].astype(F32) * (dqk ** -0.5)
        vext = jnp.concatenate([v_ref[:, h * dv:(h + 1) * dv], ones_col], axis=1)
        b_col = bcum[:, h:h + 1]
        i_col = gt[:, h:h + 1]
        m_st = m_ref[h:h + 1, 0:1]
        log_d = jnp.where(causal, b_col - rj_t[h:h + 1, :], -jnp.inf)
        log_inter = b_col + m_st
        m_t = jnp.maximum(log_inter, jnp.max(log_d, axis=-1, keepdims=True))
        dmat = jnp.exp(log_d - m_t)
        w_inter = jnp.exp(log_inter - m_t)
        sc = _mm_nt(q, k) * dmat
        c_st = c_ref[h]
        num = _mm(sc, vext) + w_inter * _mm(q, c_st)
        dot = num[:, dv:dv + 1]
        hc = num[:, :dv] / jnp.maximum(jnp.abs(dot), jnp.exp(-m_t))
        m_new = m_t[lc - 1:lc, :]
        b_last = b_col[lc - 1:lc, :]
        w_s = jnp.exp(b_last - b_col + i_col - m_new)
        dec = jnp.exp(b_last + m_st - m_new)
        c_ref[h] = dec * c_st + _mm_tn(k * w_s, vext)
        m_ref[h:h + 1, :] = jnp.broadcast_to(m_new, (1, LANES))
        yf = hc * lax.rsqrt(jnp.mean(hc * hc, axis=-1, keepdims=True) + NORM_EPS)
        og = og_ref[:, h * dv:(h + 1) * dv].astype(F32)
        o_ref[:, h * dv:(h + 1) * dv] = (_sigmoid(og) * (yf * hn_ref[:, h * dv:(h + 1) * dv])).astype(BF16)


def _mlscan(proj, gates, bif, hn_w, batch, seq, d, lc=256):
    t = proj.shape[0]
    nc = seq // lc
    dq = d // 2
    tok = lambda bi, ci: bi * nc + ci
    return pl.pallas_call(
        _mlscan_kernel,
        grid=(batch, nc),
        in_specs=[
            pl.BlockSpec((lc, dq), lambda bi, ci: (tok(bi, ci), 0)),
            pl.BlockSpec((lc, dq), lambda bi, ci: (tok(bi, ci), 1)),
            pl.BlockSpec((lc, d), lambda bi, ci: (tok(bi, ci), 1)),
            pl.BlockSpec((lc, d), lambda bi, ci: (tok(bi, ci), 2)),
            pl.BlockSpec((lc, LANES), lambda bi, ci: (tok(bi, ci), 0)),
            pl.BlockSpec((1, LANES), lambda bi, ci: (0, 0)),
            pl.BlockSpec((1, d), lambda bi, ci: (0, 0)),
        ],
        out_specs=pl.BlockSpec((lc, d), lambda bi, ci: (tok(bi, ci), 0)),
        out_shape=jax.ShapeDtypeStruct((t, d), BF16),
        scratch_shapes=[pltpu.VMEM((ML_HEADS, dq // ML_HEADS, d // ML_HEADS + LANES), F32),
                        pltpu.VMEM((SUBLANES, LANES), F32)],
        compiler_params=_params("parallel", "arbitrary"),
        name="mlscan",
    )(proj, proj, proj, proj, gates, bif, hn_w)


def _pad_cols(w, n):
    return jnp.pad(w, ((0, 0), (0, n - w.shape[1])))


def _pad_rows(w, n):
    return jnp.pad(w, ((0, n - w.shape[0]), (0, 0)))


def kernel(x, positions, norm_mix, norm_ffn, norm_final, rw_mu, rw_w_rkv, rw_w0, rw_w1, rw_w2, rw_a0, rw_a1, rw_a2, rw_g1, rw_g2, rw_k_k, rw_k_a, rw_r_k, rw_ln_w, rw_ln_b, rw_w_o, rw_v0, rw_v1, rw_v2, ret_w_in, ret_gn_w, ret_w_out, ml_w_in, ml_b_if, ml_hn_w, ml_w_out, ffn_w1, ffn_w3, ffn_w2):
    batch, seq, d = x.shape
    t = batch * seq
    depth = norm_mix.shape[0]
    xf = x.reshape(t, d)
    row = lambda vec: vec.reshape(1, -1).astype(F32)

    dk = d // RET_HEADS
    inv_freq = (1.0 / (ROPE_BASE ** jnp.linspace(0.0, 1.0, dk // 2, dtype=F32))).reshape(1, -1)
    cos, sin = _rope_tables(positions.reshape(t, 1), inv_freq)
    log_gamma = jnp.log1p(-jnp.exp2(-5.0 - jnp.arange(RET_HEADS, dtype=F32)))
    lg = jnp.broadcast_to(log_gamma[:, None, None], (RET_HEADS, 1, LANES))

    ffn_w = (ffn_w1[0].astype(BF16), ffn_w3[0].astype(BF16), ffn_w2[0].astype(BF16))
    rw_w_rkv_b = rw_w_rkv.astype(BF16)

    v_first = None
    for i in range(depth):
        kind, j = i % 3, i // 3
        g_mix = row(norm_mix[i])
        if kind == 0:
            has_vres = j > 0
            zeros_v1 = jnp.zeros((d, RW_LORA_G - RW_LORA_V), F32)
            zeros_v2 = jnp.zeros((RW_LORA_G - RW_LORA_V, d), F32)
            lin = jnp.concatenate([
                _pad_cols(rw_w1[j], RW_LORA_A - RW_LORA_W), _pad_cols(rw_a1[j], RW_LORA_V - RW_LORA_A),
                _pad_cols(rw_v1[j - 1], RW_LORA_G - RW_LORA_V) if has_vres else zeros_v1,
                rw_g1[j]], axis=1).astype(BF16)
            lout = jnp.concatenate([
                _pad_rows(rw_w2[j], RW_LORA_A - RW_LORA_W), _pad_rows(rw_a2[j], RW_LORA_V - RW_LORA_A),
                _pad_rows(rw_v2[j - 1], RW_LORA_G - RW_LORA_V) if has_vres else zeros_v2,
                rw_g2[j]], axis=0).astype(BF16)
            v0 = rw_v0[j - 1] if has_vres else jnp.zeros((d,), F32)
            zero = jnp.zeros((d,), F32)
            vecs = jnp.stack([rw_w0[j], rw_a0[j], v0, rw_k_k[j], rw_k_a[j], zero, zero, zero]).astype(F32)
            r, lw, k, v, a, b, g = _rwproj(xf, g_mix, rw_mu[j].astype(F32), rw_w_rkv_b, j, lin, lout,
                                           vecs, v_first if has_vres else None, seq)
            if j == 0:
                v_first = v
            svecs = jnp.stack([rw_r_k[j].reshape(-1), rw_ln_w[j], rw_ln_b[j], zero, zero, zero, zero, zero]).astype(F32)
            z = _rwscan(r, lw, k, v, a, b, g, svecs, batch, seq)
            xf = _outproj(z, rw_w_o[j].astype(BF16), xf)
        elif kind == 1:
            (proj,) = _normproj(xf, g_mix, ret_w_in[j].astype(BF16), tn=1024)
            z = _retscan(proj, cos, sin, lg, row(ret_gn_w[j]), batch, seq, d)
            xf = _outproj(z, ret_w_out[j].astype(BF16), xf)
        else:
            n_pad = 3 * d + 2 * LANES
            tn = n_pad // 5
            w_in = _pad_cols(ml_w_in[j], n_pad).astype(BF16)
            gate_lo = 3 * d - (n_pad - tn)
            proj, gates = _normproj(xf, g_mix, w_in, tn=tn, side_cols=(gate_lo, gate_lo + LANES))
            bif = _pad_cols(ml_b_if[j].reshape(1, -1).astype(F32), LANES)
            z = _mlscan(proj, gates, bif, row(ml_hn_w[j]), batch, seq, d)
            xf = _outproj(z, ml_w_out[j].astype(BF16), xf)
        nxt = (ffn_w1, ffn_w3, ffn_w2, i + 1) if i + 1 < depth else None
        xf, *ffn_w = _ffn(xf, row(norm_ffn[i]), *ffn_w, row(norm_final), final_norm=(i == depth - 1),
                          next_weights=nxt)
    return xf.reshape(batch, seq, d)
```

```python
import functools

import jax
import jax.numpy as jnp
from jax import lax
from jax.experimental import pallas as pl
from jax.experimental.pallas import tpu as pltpu

F32 = jnp.float32
BF16 = jnp.bfloat16

NORM_EPS = 1e-6
RW_HEAD = 64
RW_GN_EPS = 64e-5
RW_CHUNK = 64
RW_INV_BASE = 8
RET_HEADS = 8
ROPE_BASE = 10000.0
ML_HEADS = 4
ML_GATE_CAP = 15.0
LANES = 128
SUBLANES = 8
NORM_ROWS = 256
VMEM_LIMIT = 56 * 1024 * 1024


def _params(*sem):
    return pltpu.CompilerParams(dimension_semantics=sem, vmem_limit_bytes=VMEM_LIMIT)


def _mm(a, b):
    return jnp.dot(a.astype(BF16), b.astype(BF16), preferred_element_type=F32)


def _mm_nt(a, b):
    return lax.dot_general(a.astype(BF16), b.astype(BF16), (((1,), (1,)), ((), ())),
                           preferred_element_type=F32)


def _mm_tn(a, b):
    return lax.dot_general(a.astype(BF16), b.astype(BF16), (((0,), (0,)), ((), ())),
                           preferred_element_type=F32)


def _rms(x, g):
    return x * lax.rsqrt(jnp.mean(x * x, axis=-1, keepdims=True) + NORM_EPS) * g


def _sigmoid(x):
    return 0.5 * jnp.tanh(0.5 * x) + 0.5


def _softplus(x):
    return jnp.maximum(x, 0.0) + jnp.log(1.0 + jnp.exp(-jnp.abs(x)))


def _cumsum_rows(x):
    n = x.shape[0]
    row = lax.broadcasted_iota(jnp.int32, x.shape, 0)
    s = 1
    while s < n:
        x = x + jnp.where(row >= s, pltpu.roll(x, s, 0), 0.0)
        s *= 2
    return x


def _seg64_sum_lanes(x):
    first = lax.broadcasted_iota(jnp.int32, x.shape, 1) < RW_HEAD
    s0 = jnp.sum(jnp.where(first, x, 0.0), axis=1, keepdims=True)
    s1 = jnp.sum(jnp.where(first, 0.0, x), axis=1, keepdims=True)
    return jnp.where(first, s0, s1)


def _seg64_sum(x, split=True):
    r = lax.broadcasted_iota(jnp.int32, (LANES, LANES), 0) // RW_HEAD
    c = lax.broadcasted_iota(jnp.int32, (LANES, LANES), 1) // RW_HEAD
    bd = jnp.where(r == c, 1.0, 0.0).astype(BF16)
    hi = x.astype(BF16)
    out = jnp.dot(hi, bd, preferred_element_type=F32)
    if split:
        out = out + jnp.dot((x - hi.astype(F32)).astype(BF16), bd, preferred_element_type=F32)
    return out


def _ffn_kernel(*refs, final_norm, cast_next):
    if cast_next:
        (x_ref, g_ref, w1_ref, w3_ref, w2_ref, gf_ref, n1_ref, n3_ref, n2_ref,
         o_ref, n1_o, n3_o, n2_o, h_s) = refs
        n1_o[...] = n1_ref[...].astype(BF16)
        n3_o[...] = n3_ref[...].astype(BF16)
        n2_o[...] = n2_ref[...].astype(BF16)
    else:
        x_ref, g_ref, w1_ref, w3_ref, w2_ref, gf_ref, o_ref, h_s = refs
    j = pl.program_id(1)
    tm = x_ref.shape[0]
    half = w1_ref.shape[1] // 2

    @pl.when(j == 0)
    def _():
        for s in range(0, tm, NORM_ROWS):
            x = x_ref[s:s + NORM_ROWS, :]
            h_s[s:s + NORM_ROWS, :] = _rms(x, g_ref[...]).astype(BF16)
            o_ref[s:s + NORM_ROWS, :] = x

    h = h_s[...]
    gate_up = [(jnp.dot(h, w1_ref[:, c:c + half], preferred_element_type=F32),
                jnp.dot(h, w3_ref[:, c:c + half], preferred_element_type=F32)) for c in (0, half)]
    down = None
    for n, (a, b) in enumerate(gate_up):
        act = (a * _sigmoid(a) * b).astype(BF16)
        part = jnp.dot(act, w2_ref[n * half:(n + 1) * half, :], preferred_element_type=F32)
        down = part if down is None else down + part
    o_ref[...] += down

    if final_norm:
        @pl.when(j == pl.num_programs(1) - 1)
        def _():
            for s in range(0, tm, NORM_ROWS):
                o_ref[s:s + NORM_ROWS, :] = _rms(o_ref[s:s + NORM_ROWS, :], gf_ref[...])


def _ffn(x, g, w1, w3, w2, gf, final_norm, next_weights=None, tm=1024, tf=512):
    t, d = x.shape
    ff = w1.shape[1]
    tf = min(tf, ff)
    ni = t // tm
    in_specs = [
        pl.BlockSpec((tm, d), lambda i, j: (i, 0), pipeline_mode=pl.Buffered(1)),
        pl.BlockSpec((1, d), lambda i, j: (0, 0)),
        pl.BlockSpec((d, tf), lambda i, j: (0, j)),
        pl.BlockSpec((d, tf), lambda i, j: (0, j)),
        pl.BlockSpec((tf, d), lambda i, j: (j, 0)),
        pl.BlockSpec((1, d), lambda i, j: (0, 0)),
    ]
    out_specs = [pl.BlockSpec((tm, d), lambda i, j: (i, 0))]
    out_shape = [jax.ShapeDtypeStruct((t, d), F32)]
    args = [x, g, w1, w3, w2, gf]
    if next_weights is not None:
        w1s, w3s, w2s, layer = next_weights
        rb = d // ni
        assert rb * ni == d and rb % LANES == 0
        in_specs += [pl.BlockSpec((None, rb, tf), lambda i, j: (layer, i, j)),
                     pl.BlockSpec((None, rb, tf), lambda i, j: (layer, i, j)),
                     pl.BlockSpec((None, tf, rb), lambda i, j: (layer, j, i))]
        out_specs += [pl.BlockSpec((rb, tf), lambda i, j: (i, j)),
                      pl.BlockSpec((rb, tf), lambda i, j: (i, j)),
                      pl.BlockSpec((tf, rb), lambda i, j: (j, i))]
        out_shape += [jax.ShapeDtypeStruct((d, ff), BF16), jax.ShapeDtypeStruct((d, ff), BF16),
                      jax.ShapeDtypeStruct((ff, d), BF16)]
        args += [w1s, w3s, w2s]
    return pl.pallas_call(
        functools.partial(_ffn_kernel, final_norm=final_norm, cast_next=next_weights is not None),
        grid=(ni, ff // tf),
        in_specs=in_specs,
        out_specs=out_specs,
        out_shape=out_shape,
        scratch_shapes=[pltpu.VMEM((tm, d), BF16)],
        compiler_params=_params("parallel", "arbitrary"),
        name="ffn",
    )(*args)


def _outproj_kernel(z_ref, w_ref, x_ref, o_ref):
    o_ref[...] = x_ref[...] + jnp.dot(z_ref[...], w_ref[...], preferred_element_type=F32)


OUTPROJ_W_BYTES = 8 * 1024 * 1024


def _outproj(z, w, x, tm=512):
    t, k = z.shape
    d = w.shape[1]
    tn = min(d, OUTPROJ_W_BYTES // (2 * k))
    return pl.pallas_call(
        _outproj_kernel,
        grid=(d // tn, t // tm),
        in_specs=[
            pl.BlockSpec((tm, k), lambda j, i: (i, 0)),
            pl.BlockSpec((k, tn), lambda j, i: (0, j)),
            pl.BlockSpec((tm, tn), lambda j, i: (i, j)),
        ],
        out_specs=pl.BlockSpec((tm, tn), lambda j, i: (i, j)),
        out_shape=jax.ShapeDtypeStruct((t, d), F32),
        compiler_params=_params("arbitrary", "arbitrary"),
        name="outproj",
    )(z, w, x)


def _normproj_kernel(x_ref, g_ref, w_ref, o_ref, *rest, side_cols):
    h_s = rest[-1]

    @pl.when(pl.program_id(1) == 0)
    def _():
        for s in range(0, x_ref.shape[0], NORM_ROWS):
            h_s[s:s + NORM_ROWS, :] = _rms(x_ref[s:s + NORM_ROWS, :], g_ref[...]).astype(BF16)

    acc = jnp.dot(h_s[...], w_ref[...], preferred_element_type=F32)
    o_ref[...] = acc.astype(o_ref.dtype)
    if side_cols:
        @pl.when(pl.program_id(1) == pl.num_programs(1) - 1)
        def _():
            rest[0][...] = acc[:, side_cols[0]:side_cols[1]]


def _normproj(x, g, w, tn, side_cols=None, tm=1024):
    t, d = x.shape
    n = w.shape[1]
    out_specs = [pl.BlockSpec((tm, tn), lambda i, j: (i, j))]
    out_shape = [jax.ShapeDtypeStruct((t, n), BF16)]
    if side_cols:
        width = side_cols[1] - side_cols[0]
        out_specs.append(pl.BlockSpec((tm, width), lambda i, j: (i, 0)))
        out_shape.append(jax.ShapeDtypeStruct((t, width), F32))
    return pl.pallas_call(
        functools.partial(_normproj_kernel, side_cols=side_cols),
        grid=(t // tm, n // tn),
        in_specs=[
            pl.BlockSpec((tm, d), lambda i, j: (i, 0)),
            pl.BlockSpec((1, d), lambda i, j: (0, 0)),
            pl.BlockSpec((d, tn), lambda i, j: (0, j)),
        ],
        out_specs=out_specs,
        out_shape=out_shape,
        scratch_shapes=[pltpu.VMEM((tm, d), BF16)],
        compiler_params=_params("parallel", "arbitrary"),
        name="normproj",
    )(x, g, w)


RW_LORA_W = 0
RW_LORA_A = 128
RW_LORA_V = 256
RW_LORA_G = 384
RW_LORA_END = 640
RW_PROLOGUE_ROWS = 128


def _rwproj_kernel(*refs, tiles_per_seq, has_vres):
    if has_vres:
        (x_ref, xp_ref, g_ref, mu_ref, wrkv_ref, lin_ref, lout_ref, vec_ref, vf_ref,
         r_o, lw_o, k_o, v_o, a_o, b_o, g_o, xr_s, xk_s, xv_s, l1_s) = refs
    else:
        (x_ref, xp_ref, g_ref, mu_ref, wrkv_ref, lin_ref, lout_ref, vec_ref,
         r_o, lw_o, k_o, v_o, a_o, b_o, g_o, xr_s, xk_s, xv_s, l1_s) = refs
        vf_ref = None
    i = pl.program_id(0)

    @pl.when(pl.program_id(1) == 0)
    def _():
        g = g_ref[...]
        mu = mu_ref[...]
        sub = RW_PROLOGUE_ROWS
        for s in range(x_ref.shape[0] // sub):
            rows = slice(s * sub, (s + 1) * sub)
            h = _rms(x_ref[rows, :], g)
            if s == 0:
                hp = _rms(xp_ref[SUBLANES - 1:SUBLANES, :], g)
                hp = jnp.where(i % tiles_per_seq == 0, 0.0, hp)
            else:
                hp = _rms(x_ref[s * sub - SUBLANES:s * sub, :], g)[SUBLANES - 1:SUBLANES, :]
            row = lax.broadcasted_iota(jnp.int32, h.shape, 0)
            xx = jnp.where(row == 0, hp, pltpu.roll(h, 1, 0)) - h
            xr_s[rows, :] = (h + xx * mu[0:1]).astype(BF16)
            xk_s[rows, :] = (h + xx * mu[2:3]).astype(BF16)
            xv = (h + xx * mu[3:4]).astype(BF16)
            xv_s[rows, :] = xv
            xw = h + xx * mu[1:2]
            xa = h + xx * mu[4:5]
            xg = h + xx * mu[5:6]
            l1_s[rows, RW_LORA_W:RW_LORA_A] = jnp.tanh(_mm(xw, lin_ref[:, RW_LORA_W:RW_LORA_A])).astype(BF16)
            l1_s[rows, RW_LORA_A:RW_LORA_V] = _mm(xa, lin_ref[:, RW_LORA_A:RW_LORA_V]).astype(BF16)
            l1_s[rows, RW_LORA_V:RW_LORA_G] = _mm(xv, lin_ref[:, RW_LORA_V:RW_LORA_G]).astype(BF16)
            l1_s[rows, RW_LORA_G:RW_LORA_END] = _sigmoid(
                _mm(xg, lin_ref[:, RW_LORA_G:RW_LORA_END])).astype(BF16)

    r = jnp.dot(xr_s[...], wrkv_ref[0], preferred_element_type=F32)
    k = jnp.dot(xk_s[...], wrkv_ref[1], preferred_element_type=F32)
    v = jnp.dot(xv_s[...], wrkv_ref[2], preferred_element_type=F32)
    vec = vec_ref[...]
    w0, a0, v0, k_k, k_a = (vec[n:n + 1] for n in range(5))

    def lora2(lo, hi):
        return jnp.dot(l1_s[:, lo:hi], lout_ref[lo:hi, :], preferred_element_type=F32)

    w_log = -_softplus(-(w0 + lora2(RW_LORA_W, RW_LORA_A))) - 0.5
    lw_o[...] = -jnp.exp(w_log)
    a = _sigmoid(a0 + lora2(RW_LORA_A, RW_LORA_V))
    if has_vres:
        v = v + (vf_ref[...].astype(F32) - v) * _sigmoid(v0 + lora2(RW_LORA_V, RW_LORA_G))
    g_o[...] = lora2(RW_LORA_G, RW_LORA_END).astype(g_o.dtype)
    kk = k * k_k
    tn = kk.shape[1]
    ss = jnp.concatenate([_seg64_sum_lanes(jnp.square(kk[:, c:c + LANES])) for c in range(0, tn, LANES)], axis=1)
    kk = kk * lax.rsqrt(jnp.maximum(ss, 1e-24))
    r_o[...] = r.astype(r_o.dtype)
    k_o[...] = (k * (1.0 + (a - 1.0) * k_a)).astype(k_o.dtype)
    v_o[...] = v.astype(v_o.dtype)
    a_o[...] = (-kk).astype(a_o.dtype)
    b_o[...] = (kk * a).astype(b_o.dtype)


def _rwproj(x, g, mu, wrkv, layer, lin, lout, vecs, vfirst, seq, tm=512, tn=512):
    t, d = x.shape
    has_vres = vfirst is not None
    rows8 = tm // SUBLANES
    in_specs = [
        pl.BlockSpec((tm, d), lambda i, j: (i, 0)),
        pl.BlockSpec((SUBLANES, d), lambda i, j: (jnp.maximum(i * rows8 - 1, 0), 0)),
        pl.BlockSpec((1, d), lambda i, j: (0, 0)),
        pl.BlockSpec((6, d), lambda i, j: (0, 0)),
        pl.BlockSpec((None, 3, d, tn), lambda i, j: (layer, 0, 0, j)),
        pl.BlockSpec((d, RW_LORA_END), lambda i, j: (0, 0)),
        pl.BlockSpec((RW_LORA_END, tn), lambda i, j: (0, j)),
        pl.BlockSpec((SUBLANES, tn), lambda i, j: (0, j)),
    ]
    args = [x, x, g, mu, wrkv, lin, lout, vecs]
    if has_vres:
        in_specs.append(pl.BlockSpec((tm, tn), lambda i, j: (i, j)))
        args.append(vfirst)
    out_dtypes = [BF16, F32, BF16, BF16, BF16, BF16, BF16]
    return pl.pallas_call(
        functools.partial(_rwproj_kernel, tiles_per_seq=seq // tm, has_vres=has_vres),
        grid=(t // tm, d // tn),
        in_specs=in_specs,
        out_specs=[pl.BlockSpec((tm, tn), lambda i, j: (i, j))] * 7,
        out_shape=[jax.ShapeDtypeStruct((t, d), dt) for dt in out_dtypes],
        scratch_shapes=[pltpu.VMEM((tm, d), BF16)] * 3 + [pltpu.VMEM((tm, RW_LORA_END), BF16)],
        compiler_params=_params("parallel", "arbitrary"),
        name="rwproj",
    )(*args)


def _rwscan_kernel(r_ref, lw_ref, k_ref, v_ref, a_ref, b_ref, g_ref, vec_ref, o_ref,
                   s_ref, rp_st, yp_st, mt_st, ct_st, pl_st, bg_st, g_st, *, chunks):
    cl = RW_CHUNK
    pair = 2 * cl
    pairs = lw_ref.shape[1] // LANES
    ti = pl.program_id(2)
    slot_w = ti % 2
    slot_r = 1 - slot_w

    @pl.when(ti == 0)
    def _():
        s_ref[...] = jnp.zeros_like(s_ref)
        for ref in (rp_st, yp_st, mt_st, ct_st, pl_st, bg_st, g_st):
            ref[1] = jnp.zeros(ref.shape[1:], ref.dtype)

    head0 = lax.broadcasted_iota(jnp.int32, (cl, LANES), 1) < RW_HEAD

    def stack(x):
        return jnp.concatenate([jnp.where(head0, x, 0.0), jnp.where(head0, 0.0, x)], axis=0).astype(BF16)

    vec = vec_ref[...]
    states = [s_ref[p] for p in range(pairs)]
    ys = [[] for _ in range(pairs)]
    links_done = [0]

    def serial_link():
        j = links_done[0]
        if j >= chunks:
            return
        links_done[0] = j + 1
        for p in range(pairs):
            c = p * chunks + j
            st = states[p]
            ys[p].append(_mm_nt(rp_st[slot_r, c], stack(st)) + yp_st[slot_r, c])
            states[p] = jnp.exp(pl_st[slot_r, c][0:1]) * st + _mm(st, mt_st[slot_r, c]) + ct_st[slot_r, c]

    norm = {}

    def finish_mean():
        while links_done[0] < chunks:
            serial_link()
        for p in range(pairs):
            s_ref[p] = states[p]
        norm["y"] = [jnp.concatenate(ys[p], axis=0) for p in range(pairs)]
        norm["mean"] = [_seg64_sum_lanes(y) * (1.0 / RW_HEAD) for y in norm["y"]]

    def finish_var():
        norm["yc"] = [norm["y"][p] - norm["mean"][p] for p in range(pairs)]
        norm["var"] = [_seg64_sum_lanes(yc * yc) * (1.0 / RW_HEAD) for yc in norm["yc"]]

    def finish_store():
        for p in range(pairs):
            lanes = slice(p * LANES, (p + 1) * LANES)
            yn = norm["yc"][p] * lax.rsqrt(norm["var"][p] + RW_GN_EPS) * vec[1:2, lanes] + vec[2:3, lanes]
            o_ref[:, lanes] = (yn * g_st[slot_r, :, lanes] + bg_st[slot_r, :, lanes]).astype(BF16)

    pending = [finish_mean, finish_var, finish_store]

    def advance():
        if links_done[0] < chunks:
            serial_link()
        elif pending:
            pending.pop(0)()

    advance()

    ri = lax.broadcasted_iota(jnp.int32, (cl, LANES), 0)
    ci = lax.broadcasted_iota(jnp.int32, (cl, LANES), 1) % cl
    strict = ri > ci
    incl = ri >= ci
    eye = jnp.where(ri == ci, 1.0, 0.0)
    rb = lax.broadcasted_iota(jnp.int32, (pair, LANES), 0) // cl
    cb = lax.broadcasted_iota(jnp.int32, (pair, LANES), 1) // RW_HEAD
    same_head = rb == cb

    cs = range(pairs * chunks)
    rows = [(slice((c % chunks) * cl, (c % chunks + 1) * cl),
             slice((c // chunks) * LANES, (c // chunks + 1) * LANES)) for c in cs]

    def prepare(c):
        tok, lanes = rows[c]
        lw = lw_ref[tok, lanes]
        cum = lw
        s = 1
        while s < cl:
            cum = cum + jnp.where(ri >= s, pltpu.roll(cum, s, 0), 0.0)
            s *= 2
        r, k, v = (ref[tok, lanes].astype(F32) for ref in (r_ref, k_ref, v_ref))
        a, b, g = (ref[tok, lanes].astype(F32) for ref in (a_ref, b_ref, g_ref))
        bonus = _seg64_sum_lanes(r * k * vec[0:1, lanes])
        bg_st[slot_w, tok, lanes] = bonus * v * g
        g_st[slot_w, tok, lanes] = g
        last = cum[cl - 1:cl, :]
        e_neg = jnp.exp(-cum)
        tail = jnp.exp(last - cum)
        at = a * jnp.exp(cum - lw)
        rt = r * jnp.exp(cum)
        gram = _mm_nt(jnp.concatenate([at, rt], axis=0),
                      jnp.concatenate([stack(b * e_neg), stack(k * e_neg)], axis=0))
        return dict(at=at, rt=rt, last=last, gram=gram, v_c=v.astype(BF16), v_s=stack(v),
                    bl_c=(b * tail).astype(BF16), kl_c=(k * tail).astype(BF16))

    items = []
    for c in cs:
        items.append(prepare(c))
        if c % 4 == 3:
            advance()
    at, rt, last, gram = ([it[name] for it in items] for name in ("at", "rt", "last", "gram"))
    v_c, v_s, bl_c, kl_c = ([it[name] for it in items] for name in ("v_c", "v_s", "bl_c", "kl_c"))
    n_ab = [jnp.where(strict, gram[c][:cl, :LANES], 0.0) for c in cs]
    a_ak = [jnp.where(strict, gram[c][:cl, LANES:], 0.0) for c in cs]
    a_r = [jnp.concatenate([jnp.where(incl, gram[c][cl:, :LANES], 0.0),
                            jnp.where(incl, gram[c][cl:, LANES:], 0.0)], axis=1).astype(BF16) for c in cs]
    x1 = [_mm(a_ak[c], v_s[c]) for c in cs]
    advance()
    blk_r, blk_c = ri // RW_INV_BASE, ci // RW_INV_BASE
    n_d = [jnp.where(blk_r == blk_c, n_ab[c], 0.0) for c in cs]
    tinv = [eye + n_d[c] for c in cs]
    pw = [_mm(n_d[c], stack(n_d[c])) for c in cs]
    advance()
    span = 4
    while span < RW_INV_BASE:
        both = [_mm(jnp.concatenate([pw[c], tinv[c]], axis=0), stack(pw[c])) for c in cs]
        pw = [both[c][:cl] for c in cs]
        tinv = [tinv[c] + both[c][cl:] for c in cs]
        span *= 2
        advance()
    tinv = [tinv[c] + _mm(tinv[c], stack(pw[c])) for c in cs]
    size = RW_INV_BASE
    while size < cl:
        lower_left = ((ri // (2 * size)) == (ci // (2 * size))) & ((ri // size) == (ci // size) + 1)
        advance()
        e = [_mm(jnp.where(lower_left, n_ab[c], 0.0), stack(tinv[c])) for c in cs]
        tinv = [tinv[c] + _mm(tinv[c], stack(e[c])) for c in cs]
        size *= 2
    advance()
    tz = [_mm(tinv[c], jnp.concatenate([stack(at[c]), stack(x1[c])], axis=1)) for c in cs]
    advance()
    zeros = jnp.zeros((pair, LANES), BF16)
    q = [_mm(a_r[c], jnp.concatenate(
        [jnp.concatenate([stack(tz[c][:, :LANES]), stack(tz[c][:, LANES:])], axis=1),
         jnp.concatenate([zeros, v_s[c]], axis=1)], axis=0)) for c in cs]
    while links_done[0] < chunks or pending:
        advance()
    mt =[jnp.where(same_head, _mm_tn(tz[c][:, :LANES], bl_c[c]), 0.0) for c in cs]
    ct = [_mm_tn(jnp.concatenate([tz[c][:, LANES:].astype(BF16), v_c[c]], axis=0),
                 jnp.concatenate([bl_c[c], kl_c[c]], axis=0)) for c in cs]

    for c in cs:
        rp_st[slot_w, c] = (rt[c] + q[c][:, :LANES]).astype(BF16)
        yp_st[slot_w, c] = q[c][:, LANES:]
        mt_st[slot_w, c] = mt[c].astype(BF16)
        ct_st[slot_w, c] = jnp.where(head0, ct[c][:cl], ct[c][cl:])
        pl_st[slot_w, c] = jnp.broadcast_to(last[c], (SUBLANES, LANES))


def _rwscan(r, lw, k, v, a, b, g, vecs, batch, seq, lt=256, pairs=4):
    t, d = r.shape
    nt = seq // lt
    width = pairs * LANES
    items = pairs * (lt // RW_CHUNK)
    blk_in = pl.BlockSpec((lt, width), lambda bi, hp, ti: (bi * nt + jnp.minimum(ti, nt - 1), hp))
    blk_out = pl.BlockSpec((lt, width), lambda bi, hp, ti: (bi * nt + jnp.maximum(ti - 1, 0), hp))
    return pl.pallas_call(
        functools.partial(_rwscan_kernel, chunks=lt // RW_CHUNK),
        grid=(batch, d // width, nt + 1),
        in_specs=[blk_in] * 7 + [pl.BlockSpec((SUBLANES, width), lambda bi, hp, ti: (0, hp))],
        out_specs=blk_out,
        out_shape=jax.ShapeDtypeStruct((t, d), BF16),
        scratch_shapes=[
            pltpu.VMEM((pairs, RW_HEAD, LANES), F32),
            pltpu.VMEM((2, items, RW_CHUNK, LANES), BF16),
            pltpu.VMEM((2, items, RW_CHUNK, LANES), F32),
            pltpu.VMEM((2, items, LANES, LANES), BF16),
            pltpu.VMEM((2, items, RW_HEAD, LANES), F32),
            pltpu.VMEM((2, items, 8, LANES), F32),
            pltpu.VMEM((2, lt, width), F32),
            pltpu.VMEM((2, lt, width), F32),
        ],
        compiler_params=_params("parallel", "parallel", "arbitrary"),
        name="rwscan",
    )(r, lw, k, v, a, b, g, vecs)


def _rope_kernel(pos_ref, inv_ref, cos_ref, sin_ref):
    ang = pos_ref[...].astype(F32) * inv_ref[...]
    cos_ref[...] = jnp.cos(ang)
    sin_ref[...] = jnp.sin(ang)


def _rope_tables(pos, inv_freq, tm=512):
    t = pos.shape[0]
    half = inv_freq.shape[1]
    out = jax.ShapeDtypeStruct((t, half), F32)
    return pl.pallas_call(
        _rope_kernel,
        grid=(t // tm,),
        in_specs=[pl.BlockSpec((tm, 1), lambda i: (i, 0)), pl.BlockSpec((1, half), lambda i: (0, 0))],
        out_specs=[pl.BlockSpec((tm, half), lambda i: (i, 0))] * 2,
        out_shape=[out, out],
        compiler_params=_params("parallel"),
        name="rope",
    )(pos, inv_freq)


RET_HEADS_PER_STEP = 4


def _retscan_kernel(q_ref, k_ref, v_ref, gate_ref, cos_ref, sin_ref, lg_ref, gn_ref, o_ref, st_ref, intra_ref):
    lc = q_ref.shape[0]
    heads = lg_ref.shape[0]
    dk = q_ref.shape[1] // heads
    dv = v_ref.shape[1] // heads
    half = dk // 2
    ic = lax.broadcasted_iota(jnp.int32, (lc, 1), 0).astype(F32)
    lgs = [lg_ref[h][:, 0:1] for h in range(heads)]

    @pl.when(pl.program_id(2) == 0)
    def _():
        st_ref[...] = jnp.zeros_like(st_ref)
        diff = ic - lax.broadcasted_iota(jnp.int32, (1, lc), 1).astype(F32)
        causal = diff >= 0
        for h in range(heads):
            intra_ref[h] = jnp.where(causal, jnp.exp(lgs[h] * jnp.where(causal, diff, 0.0)), 0.0)

    cos, sin = cos_ref[...], sin_ref[...]

    def rot(t):
        t1, t2 = t[:, :half], t[:, half:]
        return jnp.concatenate([t1 * cos - t2 * sin, t1 * sin + t2 * cos], axis=1)

    for h in range(heads):
        lg = lgs[h]
        q = rot(q_ref[:, h * dk:(h + 1) * dk].astype(F32))
        k = rot(k_ref[:, h * dk:(h + 1) * dk].astype(F32)) * (dk ** -0.5)
        v = v_ref[:, h * dv:(h + 1) * dv]
        xi = jnp.exp(lg * (ic + 1.0))
        zeta = jnp.exp(lg * (lc - 1.0 - ic))
        st = st_ref[h]
        sc = _mm_nt(q, k) * intra_ref[h]
        y = _mm(sc, v) + _mm(q, st) * xi
        st_ref[h] = jnp.exp(lg * lc) * st + _mm_tn(k * zeta, v)
        mean = jnp.mean(y, axis=-1, keepdims=True)
        yc = y - mean
        yn = yc * lax.rsqrt(jnp.mean(yc * yc, axis=-1, keepdims=True) + NORM_EPS) * gn_ref[:, h * dv:(h + 1) * dv]
        gate = gate_ref[:, h * dv:(h + 1) * dv].astype(F32)
        o_ref[:, h * dv:(h + 1) * dv] = (gate * _sigmoid(gate) * yn).astype(BF16)


def _retscan(proj, cos, sin, lg, gn_w, batch, seq, d, lc=256):
    t = proj.shape[0]
    nc = seq // lc
    hps = RET_HEADS_PER_STEP
    nhb = RET_HEADS // hps
    dk, dv = d // RET_HEADS, 2 * d // RET_HEADS
    tok = lambda bi, hb, ci: bi * nc + ci
    return pl.pallas_call(
        _retscan_kernel,
        grid=(batch, nhb, nc),
        in_specs=[
            pl.BlockSpec((lc, hps * dk), lambda bi, hb, ci: (tok(bi, hb, ci), hb)),
            pl.BlockSpec((lc, hps * dk), lambda bi, hb, ci: (tok(bi, hb, ci), nhb + hb)),
            pl.BlockSpec((lc, hps * dv), lambda bi, hb, ci: (tok(bi, hb, ci), nhb + hb)),
            pl.BlockSpec((lc, hps * dv), lambda bi, hb, ci: (tok(bi, hb, ci), 2 * nhb + hb)),
            pl.BlockSpec((lc, dk // 2), lambda bi, hb, ci: (tok(bi, hb, ci), 0)),
            pl.BlockSpec((lc, dk // 2), lambda bi, hb, ci: (tok(bi, hb, ci), 0)),
            pl.BlockSpec((hps, 1, LANES), lambda bi, hb, ci: (hb, 0, 0)),
            pl.BlockSpec((1, hps * dv), lambda bi, hb, ci: (0, hb)),
        ],
        out_specs=pl.BlockSpec((lc, hps * dv), lambda bi, hb, ci: (tok(bi, hb, ci), hb)),
        out_shape=jax.ShapeDtypeStruct((t, 2 * d), BF16),
        scratch_shapes=[pltpu.VMEM((hps, dk, dv), F32), pltpu.VMEM((hps, lc, lc), F32)],
        compiler_params=_params("parallel", "parallel", "arbitrary"),
        name="retscan",
    )(proj, proj, proj, proj, cos, sin, lg, gn_w)


def _mlscan_kernel(q_ref, k_ref, v_ref, og_ref, gt_ref, bif_ref, hn_ref, o_ref, c_ref, m_ref):
    lc = q_ref.shape[0]
    dqk = q_ref.shape[1] // ML_HEADS
    dv = v_ref.shape[1] // ML_HEADS

    @pl.when(pl.program_id(1) == 0)
    def _():
        c_ref[...] = jnp.zeros_like(c_ref)
        m_ref[...] = jnp.zeros_like(m_ref)

    gt = gt_ref[...] + bif_ref[...]
    gt = ML_GATE_CAP * jnp.tanh(gt * (1.0 / ML_GATE_CAP))
    lane = lax.broadcasted_iota(jnp.int32, gt.shape, 1)
    is_f = (lane >= ML_HEADS) & (lane < 2 * ML_HEADS)
    log_f = jnp.where(is_f, -_softplus(-gt), 0.0)
    bcum = pltpu.roll(_cumsum_rows(log_f), LANES - ML_HEADS, 1)
    rj_t = (bcum - gt).T
    ir = lax.broadcasted_iota(jnp.int32, (lc, lc), 0)
    ic = lax.broadcasted_iota(jnp.int32, (lc, lc), 1)
    causal = ir >= ic
    ones_col = jnp.where(lax.broadcasted_iota(jnp.int32, (lc, LANES), 1) == 0, 1.0, 0.0).astype(BF16)
    for h in range(ML_HEADS):
        q = q_ref[:, h * dqk:(h + 1) * dqk]
        k = k_ref[:, h * dqk:(h + 1) * dqk].astype(F32) * (dqk ** -0.5)
        vext = jnp.concatenate([v_ref[:, h * dv:(h + 1) * dv], ones_col], axis=1)
        b_col = bcum[:, h:h + 1]
        i_col = gt[:, h:h + 1]
        m_st = m_ref[h:h + 1, 0:1]
        log_d = jnp.where(causal, b_col - rj_t[h:h + 1, :], -jnp.inf)
        log_inter = b_col + m_st
        m_t = jnp.maximum(log_inter, jnp.max(log_d, axis=-1, keepdims=True))
        dmat = jnp.exp(log_d - m_t)
        w_inter = jnp.exp(log_inter - m_t)
        sc = _mm_nt(q, k) * dmat
        c_st = c_ref[h]
        num = _mm(sc, vext) + w_inter * _mm(q, c_st)
        dot = num[:, dv:dv + 1]
        hc = num[:, :dv] / jnp.maximum(jnp.abs(dot), jnp.exp(-m_t))
        m_new = m_t[lc - 1:lc, :]
        b_last = b_col[lc - 1:lc, :]
        w_s = jnp.exp(b_last - b_col + i_col - m_new)
        dec = jnp.exp(b_last + m_st - m_new)
        c_ref[h] = dec * c_st + _mm_tn(k * w_s, vext)
        m_ref[h:h + 1, :] = jnp.broadcast_to(m_new, (1, LANES))
        yf = hc * lax.rsqrt(jnp.mean(hc * hc, axis=-1, keepdims=True) + NORM_EPS)
        og = og_ref[:, h * dv:(h + 1) * dv].astype(F32)
        o_ref[:, h * dv:(h + 1) * dv] = (_sigmoid(og) * (yf * hn_ref[:, h * dv:(h + 1) * dv])).astype(BF16)


def _mlscan(proj, gates, bif, hn_w, batch, seq, d, lc=256):
    t = proj.shape[0]
    nc = seq // lc
    dq = d // 2
    tok = lambda bi, ci: bi * nc + ci
    return pl.pallas_call(
        _mlscan_kernel,
        grid=(batch, nc),
        in_specs=[
            pl.BlockSpec((lc, dq), lambda bi, ci: (tok(bi, ci), 0)),
            pl.BlockSpec((lc, dq), lambda bi, ci: (tok(bi, ci), 1)),
            pl.BlockSpec((lc, d), lambda bi, ci: (tok(bi, ci), 1)),
            pl.BlockSpec((lc, d), lambda bi, ci: (tok(bi, ci), 2)),
            pl.BlockSpec((lc, LANES), lambda bi, ci: (tok(bi, ci), 0)),
            pl.BlockSpec((1, LANES), lambda bi, ci: (0, 0)),
            pl.BlockSpec((1, d), lambda bi, ci: (0, 0)),
        ],
        out_specs=pl.BlockSpec((lc, d), lambda bi, ci: (tok(bi, ci), 0)),
        out_shape=jax.ShapeDtypeStruct((t, d), BF16),
        scratch_shapes=[pltpu.VMEM((ML_HEADS, dq // ML_HEADS, d // ML_HEADS + LANES), F32),
                        pltpu.VMEM((SUBLANES, LANES), F32)],
        compiler_params=_params("parallel", "arbitrary"),
        name="mlscan",
    )(proj, proj, proj, proj, gates, bif, hn_w)


def _pad_cols(w, n):
    return jnp.pad(w, ((0, 0), (0, n - w.shape[1])))


def _pad_rows(w, n):
    return jnp.pad(w, ((0, n - w.shape[0]), (0, 0)))


def kernel(x, positions, norm_mix, norm_ffn, norm_final, rw_mu, rw_w_rkv, rw_w0, rw_w1, rw_w2, rw_a0, rw_a1, rw_a2, rw_g1, rw_g2, rw_k_k, rw_k_a, rw_r_k, rw_ln_w, rw_ln_b, rw_w_o, rw_v0, rw_v1, rw_v2, ret_w_in, ret_gn_w, ret_w_out, ml_w_in, ml_b_if, ml_hn_w, ml_w_out, ffn_w1, ffn_w3, ffn_w2):
    batch, seq, d = x.shape
    t = batch * seq
    depth = norm_mix.shape[0]
    xf = x.reshape(t, d)
    row = lambda vec: vec.reshape(1, -1).astype(F32)

    dk = d // RET_HEADS
    inv_freq = (1.0 / (ROPE_BASE ** jnp.linspace(0.0, 1.0, dk // 2, dtype=F32))).reshape(1, -1)
    cos, sin = _rope_tables(positions.reshape(t, 1), inv_freq)
    log_gamma = jnp.log1p(-jnp.exp2(-5.0 - jnp.arange(RET_HEADS, dtype=F32)))
    lg = jnp.broadcast_to(log_gamma[:, None, None], (RET_HEADS, 1, LANES))

    ffn_w = (ffn_w1[0].astype(BF16), ffn_w3[0].astype(BF16), ffn_w2[0].astype(BF16))
    rw_w_rkv_b = rw_w_rkv.astype(BF16)

    v_first = None
    for i in range(depth):
        kind, j = i % 3, i // 3
        g_mix = row(norm_mix[i])
        if kind == 0:
            has_vres = j > 0
            zeros_v1 = jnp.zeros((d, RW_LORA_G - RW_LORA_V), F32)
            zeros_v2 = jnp.zeros((RW_LORA_G - RW_LORA_V, d), F32)
            lin = jnp.concatenate([
                _pad_cols(rw_w1[j], RW_LORA_A - RW_LORA_W), _pad_cols(rw_a1[j], RW_LORA_V - RW_LORA_A),
                _pad_cols(rw_v1[j - 1], RW_LORA_G - RW_LORA_V) if has_vres else zeros_v1,
                rw_g1[j]], axis=1).astype(BF16)
            lout = jnp.concatenate([
                _pad_rows(rw_w2[j], RW_LORA_A - RW_LORA_W), _pad_rows(rw_a2[j], RW_LORA_V - RW_LORA_A),
                _pad_rows(rw_v2[j - 1], RW_LORA_G - RW_LORA_V) if has_vres else zeros_v2,
                rw_g2[j]], axis=0).astype(BF16)
            v0 = rw_v0[j - 1] if has_vres else jnp.zeros((d,), F32)
            zero = jnp.zeros((d,), F32)
            vecs = jnp.stack([rw_w0[j], rw_a0[j], v0, rw_k_k[j], rw_k_a[j], zero, zero, zero]).astype(F32)
            r, lw, k, v, a, b, g = _rwproj(xf, g_mix, rw_mu[j].astype(F32), rw_w_rkv_b, j, lin, lout,
                                           vecs, v_first if has_vres else None, seq)
            if j == 0:
                v_first = v
            svecs = jnp.stack([rw_r_k[j].reshape(-1), rw_ln_w[j], rw_ln_b[j], zero, zero, zero, zero, zero]).astype(F32)
            z = _rwscan(r, lw, k, v, a, b, g, svecs, batch, seq)
            xf = _outproj(z, rw_w_o[j].astype(BF16), xf)
        elif kind == 1:
            (proj,) = _normproj(xf, g_mix, ret_w_in[j].astype(BF16), tn=1024)
            z = _retscan(proj, cos, sin, lg, row(ret_gn_w[j]), batch, seq, d)
            xf = _outproj(z, ret_w_out[j].astype(BF16), xf)
        else:
            n_pad = 3 * d + 2 * LANES
            tn = n_pad // 5
            w_in = _pad_cols(ml_w_in[j], n_pad).astype(BF16)
            gate_lo = 3 * d - (n_pad - tn)
            proj, gates = _normproj(xf, g_mix, w_in, tn=tn, side_cols=(gate_lo, gate_lo + LANES))
            bif = _pad_cols(ml_b_if[j].reshape(1, -1).astype(F32), LANES)
            z = _mlscan(proj, gates, bif, row(ml_hn_w[j]), batch, seq, d)
            xf = _outproj(z, ml_w_out[j].astype(BF16), xf)
        nxt = (ffn_w1, ffn_w3, ffn_w2, i + 1) if i + 1 < depth else None
        xf, *ffn_w = _ffn(xf, row(norm_ffn[i]), *ffn_w, row(norm_final), final_norm=(i == depth - 1),
                          next_weights=nxt)
    return xf.reshape(batch, seq, d)
```
